```python
import math
import jax, jax.numpy as jnp
from jax import lax
import numpy as np

D_MODEL = 2048
BATCH = 4
SEQ = 2048
DEPTH = 4
DEC_BATCH = 128
DEC_SEQ = 1
PAST_LEN = 8192
PAGE_SIZE = 128

HG_HEADS = 8
HG_DK = 128
HG_DV = 128
HG_WIDTH = HG_HEADS * HG_DV
HG_CHUNK = 64
DA_HEADS = 4
DA_HALF = 64
DA_DV = 2 * DA_HALF
DA_WIDTH = DA_HEADS * DA_DV
DA_KV_DIM = 2 * DA_HALF + DA_DV
MLA_HEADS = 4
MLA_Q_RANK = 512
MLA_KV_RANK = 256
MLA_NOPE = 128
MLA_ROPE = 64
MLA_DV = 128
MLA_WIDTH = MLA_HEADS * MLA_DV
MLA_ROW = MLA_KV_RANK + MLA_ROPE
MIX_WIDTH = HG_WIDTH + DA_WIDTH + MLA_WIDTH
D_FF = 5632
CONV_W = 3
ROPE_THETA = 10000.0
Q_BLOCK = 128
EPS = 1e-6
MASK_VALUE = -1e30
F_FLOOR = 1e-30
SPLITS = (HG_HEADS * HG_DK, HG_HEADS * HG_DK, HG_WIDTH, HG_WIDTH,
          DA_HEADS * 2 * DA_HALF, 2 * DA_HALF, DA_DV,
          MLA_Q_RANK, MLA_KV_RANK, MLA_ROPE)
PROJ_WIDTH = sum(SPLITS)

kernel_name = 'hymba_hgrn2_diffattn_mla_convffn_step'


def rmsnorm(x, g):
    xf = x.astype(jnp.float32)
    y = xf * lax.rsqrt(jnp.mean(jnp.square(xf), axis=-1, keepdims=True) + EPS)
    return (y * g.astype(jnp.float32)).astype(x.dtype)


def rope(x, pos):
    d = x.shape[-1]
    inv = ROPE_THETA ** (-jnp.arange(0, d, 2, dtype=jnp.float32) / d)
    ang = pos.astype(jnp.float32)[:, None] * inv[None, :]
    shp = (pos.shape[0],) + (1,) * (x.ndim - 3) + (d // 2,)
    cos = jnp.cos(ang).reshape(shp)
    sin = jnp.sin(ang).reshape(shp)
    xf = x.astype(jnp.float32)
    x1, x2 = xf[..., : d // 2], xf[..., d // 2:]
    return jnp.concatenate([x1 * cos - x2 * sin, x2 * cos + x1 * sin], axis=-1).astype(x.dtype)


def adaln(c, w, b):
    m = jax.nn.silu(c) @ w + b
    return jnp.split(m[:, None, :], 6, axis=-1)


def seg_softmax(scores, masks):
    s = jnp.concatenate([jnp.where(m, sc.astype(jnp.float32), MASK_VALUE)
                         for sc, m in zip(scores, masks)], axis=-1)
    p = jax.nn.softmax(s, axis=-1)
    cuts = [int(v) for v in np.cumsum([sc.shape[-1] for sc in scores])[:-1]]
    return jnp.split(p, cuts, axis=-1)


def diff_core(q1, q2, segs, masks, lam):
    scale = DA_HALF ** -0.5
    s1 = [jnp.einsum('bqhd,bkd->bhqk', q1, kv[..., :DA_HALF]) * scale for kv in segs]
    s2 = [jnp.einsum('bqhd,bkd->bhqk', q2, kv[..., DA_HALF:2 * DA_HALF]) * scale for kv in segs]
    p1 = seg_softmax(s1, masks)
    p2 = seg_softmax(s2, masks)
    return sum(jnp.einsum('bhqk,bkv->bqhv', (a - lam * b).astype(kv.dtype), kv[..., 2 * DA_HALF:])
               for a, b, kv in zip(p1, p2, segs))


def mla_core(q_lat, q_rope, segs, masks, w_uv3):
    scale = (MLA_NOPE + MLA_ROPE) ** -0.5
    s = [(jnp.einsum('bqhr,bkr->bhqk', q_lat, lat[..., :MLA_KV_RANK])
          + jnp.einsum('bqhe,bke->bhqk', q_rope, lat[..., MLA_KV_RANK:])) * scale for lat in segs]
    ps = seg_softmax(s, masks)
    o_lat = sum(jnp.einsum('bhqk,bkr->bqhr', pp.astype(lat.dtype), lat[..., :MLA_KV_RANK])
                for pp, lat in zip(ps, segs))
    return jnp.einsum('bqhr,rhv->bqhv', o_lat, w_uv3)


def attend(core, qs, segs, q_pos, k_pos):
    T = q_pos.shape[0]
    qb = math.gcd(T, Q_BLOCK)
    nb = T // qb

    def run(qblk, qp):
        masks = [qp[:, None] >= kp[None, :] for kp in k_pos]
        return core(*qblk, segs, masks)

    if nb == 1:
        return run(qs, q_pos)
    B = qs[0].shape[0]
    blocks = tuple(q.reshape(B, nb, qb, *q.shape[2:]).swapaxes(0, 1) for q in qs)
    out = lax.map(lambda a: run(a[1:], a[0]), (q_pos.reshape(nb, qb),) + blocks)
    return out.swapaxes(0, 1).reshape(B, T, *out.shape[3:])


def hgrn2_scan(q, k, logf, v, s0):
    B, T, H, DK = q.shape
    C = math.gcd(T, HG_CHUNK)
    n = T // C

    def chunks(a):
        return a.astype(jnp.float32).reshape(B, n, C, H, a.shape[-1]).transpose(1, 0, 3, 2, 4)

    causal = jnp.tril(jnp.ones((C, C), dtype=bool))[:, :, None]

    def step(S, inp):
        qb, kb, fb, vb = inp
        b = jnp.cumsum(fb, axis=2)
        o_inter = jnp.einsum('bhtk,bhkv->bhtv', qb * jnp.exp(b), S)
        diff = b[:, :, :, None, :] - b[:, :, None, :, :]
        decay = jnp.where(causal, jnp.exp(jnp.where(causal, diff, 0.0)), 0.0)
        A = jnp.einsum('bhtsk,bhsk->bhts', qb[:, :, :, None, :] * decay, kb)
        o = o_inter + jnp.einsum('bhts,bhsv->bhtv', A, vb)
        S_new = (jnp.exp(b[:, :, -1, :])[..., None] * S
                 + jnp.einsum('bhsk,bhsv->bhkv', kb * jnp.exp(b[:, :, -1:, :] - b), vb))
        return S_new, o

    S, o = lax.scan(step, s0.astype(jnp.float32), (chunks(q), chunks(k), chunks(logf), chunks(v)))
    o = o.transpose(1, 0, 3, 2, 4).reshape(B, T, H, v.shape[-1])
    return o, S


def trunk_layer(x, c, pos, p, lb, lam_init, past):
    B, T, _ = x.shape
    sh1, sc1, g1, sh2, sc2, g2 = adaln(c, p['w_ada'], p['b_ada'])
    h = rmsnorm(x, p['norm_mix_g']) * (1.0 + sc1) + sh1
    cuts = [int(v) for v in np.cumsum(SPLITS)[:-1]]
    hq, hf, hi, hg, dq, dk, dv, cq, ckv, kr = jnp.split(h @ p['w_in'], cuts, axis=-1)

    z = hf.astype(jnp.float32)
    f = lb + (1.0 - lb) * jax.nn.sigmoid(z)
    logf = jnp.log(jnp.maximum(f, F_FLOOR))
    kin = (1.0 - lb) * jax.nn.sigmoid(-z)
    s0 = jnp.zeros((B, HG_HEADS, HG_DK, HG_DV), jnp.float32) if past is None else past['hg']
    o_a, s_a = hgrn2_scan(hq.reshape(B, T, HG_HEADS, HG_DK), kin.reshape(B, T, HG_HEADS, HG_DK),
                          logf.reshape(B, T, HG_HEADS, HG_DK), hi.reshape(B, T, HG_HEADS, HG_DV), s0)
    o_a = (rmsnorm(o_a, p['hg_norm_g']).astype(x.dtype)
           * jax.nn.silu(hg.reshape(B, T, HG_HEADS, HG_DV))).reshape(B, T, HG_WIDTH)

    q = rope(rmsnorm(dq.reshape(B, T, DA_HEADS, 2, DA_HALF), p['da_qnorm_g']), pos)
    k = rope(rmsnorm(dk.reshape(B, T, 2, DA_HALF), p['da_knorm_g']), pos)
    row_da = jnp.concatenate([k.reshape(B, T, 2 * DA_HALF), dv], axis=-1)
    lv = p['da_lam'].astype(jnp.float32)
    lam = jnp.exp(jnp.sum(lv[0] * lv[1])) - jnp.exp(jnp.sum(lv[2] * lv[3])) + lam_init
    if past is None:
        segs_da, k_pos = [row_da], [pos]
    else:
        segs_da, k_pos = [past['da'], row_da], [jnp.arange(past['da'].shape[1]), pos]
    o_b = attend(lambda a, b, segs, masks: diff_core(a, b, segs, masks, lam),
                 (q[:, :, :, 0], q[:, :, :, 1]), segs_da, pos, k_pos)
    o_b = (rmsnorm(o_b, p['da_onorm_g']) * (1.0 - lam_init)).reshape(B, T, DA_WIDTH)

    qh = (rmsnorm(cq, p['mla_cq_norm_g']) @ p['mla_w_uq']).reshape(B, T, MLA_HEADS, MLA_NOPE + MLA_ROPE)
    qh = rmsnorm(qh, p['mla_qnorm_g'])
    q_rope = rope(qh[..., MLA_NOPE:], pos)
    q_lat = jnp.einsum('bthn,rhn->bthr', qh[..., :MLA_NOPE],
                       p['mla_w_uk'].reshape(MLA_KV_RANK, MLA_HEADS, MLA_NOPE))
    row_mla = jnp.concatenate([rmsnorm(ckv, p['mla_ckv_norm_g']),
                               rope(rmsnorm(kr, p['mla_knorm_g']), pos)], axis=-1)
    segs_mla = [row_mla] if past is None else [past['mla'], row_mla]
    w_uv3 = p['mla_w_uv'].reshape(MLA_KV_RANK, MLA_HEADS, MLA_DV)
    o_c = attend(lambda a, b, segs, masks: mla_core(a, b, segs, masks, w_uv3),
                 (q_lat, q_rope), segs_mla, pos, k_pos).reshape(B, T, MLA_WIDTH)

    x = x + g1 * (jnp.concatenate([o_a, o_b, o_c], axis=-1) @ p['w_out'])

    h = rmsnorm(x, p['norm_ffn_g']) * (1.0 + sc2) + sh2
    a, lin = jnp.split(h @ p['w_up'], 2, axis=-1)
    buf = jnp.zeros((B, CONV_W - 1, D_FF), a.dtype) if past is None else past['conv'].astype(a.dtype)
    ext = jnp.concatenate([buf, a], axis=1)
    conv = p['conv_b'] + sum(ext[:, j:j + T] * p['conv_w'][j] for j in range(CONV_W))
    x = x + g2 * ((jax.nn.silu(conv) * lin) @ p['w_down'])
    return x, (row_da, row_mla, s_a, ext[:, -(CONV_W - 1):])


def setup_inputs(seed: int = 0) -> dict:
    key = jax.random.key(seed)
    ks = iter(jax.random.split(key, 48))
    f32 = jnp.float32

    def nrm(shape, scale):
        return jax.random.normal(next(ks), shape, f32) * scale

    def gain(shape):
        return 1.0 + nrm(shape, 0.02)

    n_pages = PAST_LEN // PAGE_SIZE
    n_pool = (DEC_BATCH * n_pages * 5) // 4
    page_table = jax.random.permutation(next(ks), n_pool)[:DEC_BATCH * n_pages]
    page_table = page_table.reshape(DEC_BATCH, n_pages).astype(jnp.int32)
    return {
        'x_prompt': nrm((BATCH, SEQ, D_MODEL), 1.0),
        'x_sample': nrm((DEC_BATCH, DEC_SEQ, D_MODEL), 1.0),
        'cache_diff_kv': nrm((DEPTH, n_pool, PAGE_SIZE, DA_KV_DIM), 1.0),
        'cache_mla_latent': nrm((DEPTH, n_pool, PAGE_SIZE, MLA_ROW), 1.0),
        'state_hgrn': nrm((DEPTH, DEC_BATCH, HG_HEADS, HG_DK, HG_DV), 0.5),
        'state_ffn_conv': nrm((DEPTH, DEC_BATCH, CONV_W - 1, D_FF), 1.0),
        'page_table': page_table,
        'c_prompt': nrm((BATCH, D_MODEL), 1.0),
        'c_sample': nrm((DEC_BATCH, D_MODEL), 1.0),
        'norm_mix_g': gain((DEPTH, D_MODEL)),
        'norm_ffn_g': gain((DEPTH, D_MODEL)),
        'w_ada': nrm((DEPTH, D_MODEL, 6 * D_MODEL), 0.5 * D_MODEL ** -0.5),
        'b_ada': nrm((DEPTH, 6 * D_MODEL), 0.02),
        'w_in': nrm((DEPTH, D_MODEL, PROJ_WIDTH), D_MODEL ** -0.5),
        'hg_lb_logits': nrm((DEPTH, HG_HEADS * HG_DK), 0.5),
        'hg_norm_g': gain((DEPTH, HG_DV)),
        'da_lam': nrm((DEPTH, 4, DA_HALF), 0.1),
        'da_qnorm_g': gain((DEPTH, DA_HALF)),
        'da_knorm_g': gain((DEPTH, DA_HALF)),
        'da_onorm_g': gain((DEPTH, DA_DV)),
        'mla_cq_norm_g': gain((DEPTH, MLA_Q_RANK)),
        'mla_ckv_norm_g': gain((DEPTH, MLA_KV_RANK)),
        'mla_w_uq': nrm((DEPTH, MLA_Q_RANK, MLA_HEADS * (MLA_NOPE + MLA_ROPE)), MLA_Q_RANK ** -0.5),
        'mla_w_uk': nrm((DEPTH, MLA_KV_RANK, MLA_HEADS * MLA_NOPE), MLA_KV_RANK ** -0.5),
        'mla_w_uv': nrm((DEPTH, MLA_KV_RANK, MLA_HEADS * MLA_DV), MLA_KV_RANK ** -0.5),
        'mla_qnorm_g': gain((DEPTH, MLA_NOPE + MLA_ROPE)),
        'mla_knorm_g': gain((DEPTH, MLA_ROPE)),
        'w_out': nrm((DEPTH, MIX_WIDTH, D_MODEL), MIX_WIDTH ** -0.5),
        'w_up': nrm((DEPTH, D_MODEL, 2 * D_FF), D_MODEL ** -0.5),
        'conv_w': nrm((DEPTH, CONV_W, D_FF), CONV_W ** -0.5),
        'conv_b': nrm((DEPTH, D_FF), 0.02),
        'w_down': nrm((DEPTH, D_FF, D_MODEL), D_FF ** -0.5),
    }


def reference(x_prompt, x_sample, cache_diff_kv, cache_mla_latent, state_hgrn, state_ffn_conv,
              page_table, c_prompt, c_sample, norm_mix_g, norm_ffn_g, w_ada, b_ada, w_in,
              hg_lb_logits, hg_norm_g, da_lam, da_qnorm_g, da_knorm_g, da_onorm_g,
              mla_cq_norm_g, mla_ckv_norm_g, mla_w_uq, mla_w_uk, mla_w_uv, mla_qnorm_g,
              mla_knorm_g, w_out, w_up, conv_w, conv_b, w_down):
    n_dec = x_sample.shape[0]
    past_len = page_table.shape[1] * PAGE_SIZE
    pos_p = jnp.arange(x_prompt.shape[1])
    pos_s = past_len + jnp.arange(x_sample.shape[1])
    lbp = jax.nn.softmax(hg_lb_logits.astype(jnp.float32), axis=0)
    lb_all = jnp.cumsum(lbp, axis=0) - lbp[0]

    xp, xs = x_prompt, x_sample
    da_p, da_s, mla_p, mla_s, hg_p, hg_s, cv_p, cv_s = [], [], [], [], [], [], [], []
    for l in range(DEPTH):
        p = {
            'norm_mix_g': norm_mix_g[l], 'norm_ffn_g': norm_ffn_g[l],
            'w_ada': w_ada[l], 'b_ada': b_ada[l], 'w_in': w_in[l],
            'hg_norm_g': hg_norm_g[l], 'da_lam': da_lam[l],
            'da_qnorm_g': da_qnorm_g[l], 'da_knorm_g': da_knorm_g[l], 'da_onorm_g': da_onorm_g[l],
            'mla_cq_norm_g': mla_cq_norm_g[l], 'mla_ckv_norm_g': mla_ckv_norm_g[l],
            'mla_w_uq': mla_w_uq[l], 'mla_w_uk': mla_w_uk[l], 'mla_w_uv': mla_w_uv[l],
            'mla_qnorm_g': mla_qnorm_g[l], 'mla_knorm_g': mla_knorm_g[l],
            'w_out': w_out[l], 'w_up': w_up[l], 'conv_w': conv_w[l], 'conv_b': conv_b[l],
            'w_down': w_down[l],
        }
        lam_init = 0.8 - 0.6 * math.exp(-0.3 * l)
        xp, st_p = trunk_layer(xp, c_prompt, pos_p, p, lb_all[l], lam_init, None)
        past = {
            'da': cache_diff_kv[l, page_table].reshape(n_dec, past_len, DA_KV_DIM),
            'mla': cache_mla_latent[l, page_table].reshape(n_dec, past_len, MLA_ROW),
            'hg': state_hgrn[l],
            'conv': state_ffn_conv[l],
        }
        xs, st_s = trunk_layer(xs, c_sample, pos_s, p, lb_all[l], lam_init, past)
        da_p.append(st_p[0]); mla_p.append(st_p[1]); hg_p.append(st_p[2]); cv_p.append(st_p[3])
        da_s.append(st_s[0]); mla_s.append(st_s[1]); hg_s.append(st_s[2]); cv_s.append(st_s[3])

    return (xp, xs, jnp.stack(da_p), jnp.stack(da_s), jnp.stack(mla_p), jnp.stack(mla_s),
            jnp.stack(hg_p), jnp.stack(hg_s), jnp.stack(cv_p), jnp.stack(cv_s))
```

```python
import functools
import math

import jax
import jax.numpy as jnp
from jax import lax
from jax.experimental import pallas as pl
from jax.experimental.pallas import tpu as pltpu

F32 = jnp.float32
BF16 = jnp.bfloat16

D_MODEL = 2048
HG_HEADS = 8
HG_DK = 128
HG_DV = 128
DA_HEADS = 4
DA_HALF = 64
DA_DV = 128
DA_ROW = 256
MLA_HEADS = 4
MLA_Q_RANK = 512
MLA_KV_RANK = 256
MLA_NOPE = 128
MLA_ROPE = 64
MLA_DV = 128
MLA_ROW = MLA_KV_RANK + MLA_ROPE
D_FF = 5632
PAGE = 128
ROPE_THETA = 10000.0
EPS = 1e-6
NEG = -1e30
F_FLOOR = 1e-30
PROJ_WIDTH = 5696
PROJ_PAD = 6144
LANE = 128
HG_SUB = 16
VMEM_LIMIT = 48 * 1024 * 1024


def _cp(sem, vmem=VMEM_LIMIT):
    return pltpu.CompilerParams(dimension_semantics=sem, vmem_limit_bytes=vmem)


def _sigmoid(x):
    return 1.0 / (1.0 + jnp.exp(-x))


def _silu(x):
    return x * _sigmoid(x)


def _rms(x, g):
    ms = jnp.mean(x * x, axis=-1, keepdims=True)
    return x * lax.rsqrt(ms + EPS) * g


def _rms64(x, g2):
    lo = lax.broadcasted_iota(jnp.int32, x.shape, 1) < 64
    x2 = x * x
    s_lo = jnp.sum(jnp.where(lo, x2, 0.0), axis=-1, keepdims=True)
    s_hi = jnp.sum(jnp.where(lo, 0.0, x2), axis=-1, keepdims=True)
    ms = jnp.where(lo, s_lo, s_hi) * (1.0 / 64.0)
    return x * lax.rsqrt(ms + EPS) * g2


def _rope64(x, cos2, sin2):
    lane = lax.broadcasted_iota(jnp.int32, x.shape, 1)
    first = (lane % 64) < 32
    sw = jnp.where(first, pltpu.roll(x, LANE - 32, axis=1), pltpu.roll(x, 32, axis=1))
    return x * cos2 + sw * sin2


def _dot(a, b):
    return jnp.dot(a, b, preferred_element_type=F32)


def _dot_nt(a, b):
    return lax.dot_general(a, b, (((1,), (1,)), ((), ())), preferred_element_type=F32)


def _dot_tn(a, b):
    return lax.dot_general(a, b, (((0,), (0,)), ((), ())), preferred_element_type=F32)


def _lb_kernel(x_ref, o_ref):
    x = x_ref[...]
    n = x.shape[0]
    m = jnp.max(x, axis=0, keepdims=True)
    e = jnp.exp(x - m)
    p = e / jnp.sum(e, axis=0, keepdims=True)
    rows = []
    c = None
    for l in range(n):
        c = p[l:l + 1] if c is None else c + p[l:l + 1]
        rows.append(c - p[0:1])
    o_ref[...] = jnp.concatenate(rows, axis=0)


def _lower_bounds(logits):
    return pl.pallas_call(
        _lb_kernel, out_shape=jax.ShapeDtypeStruct(logits.shape, F32), name="lower_bounds")(logits)


def _ada_kernel(c_ref, w_ref, b_ref, o_ref):
    a = _silu(c_ref[...]).astype(BF16)
    o_ref[0] = _dot(a, w_ref[0].astype(BF16)) + b_ref[0]


def _ada(c_all, w_ada, b_ada):
    L, _, N = w_ada.shape
    R = c_all.shape[0]
    tn = 1024
    return pl.pallas_call(
        _ada_kernel,
        grid=(L, N // tn),
        in_specs=[pl.BlockSpec((R, D_MODEL), lambda l, j: (0, 0)),
                  pl.BlockSpec((1, D_MODEL, tn), lambda l, j: (l, 0, j)),
                  pl.BlockSpec((1, 1, tn), lambda l, j: (l, 0, j))],
        out_specs=pl.BlockSpec((1, R, tn), lambda l, j: (l, 0, j)),
        out_shape=jax.ShapeDtypeStruct((L, R, N), F32),
        compiler_params=_cp(("parallel", "parallel")), name="adaln")(c_all, w_ada, b_ada)


def _inproj_kernel(x_ref, g_ref, sh_ref, sc_ref, w_ref, o_ref, h_scr):
    @pl.when(pl.program_id(1) == 0)
    def _():
        h = _rms(x_ref[...], g_ref[0]) * (1.0 + sc_ref[0]) + sh_ref[0]
        h_scr[...] = h.astype(BF16)

    o_ref[...] = _dot(h_scr[...], w_ref[0])


def _inproj(x, g, mod, mod_idx, mod_rows, w, l, tm):
    M = x.shape[0]
    NP = w.shape[2]
    tn = 512
    D = D_MODEL
    return pl.pallas_call(
        _inproj_kernel,
        grid=(M // tm, NP // tn),
        in_specs=[pl.BlockSpec((tm, D), lambda i, j: (i, 0)),
                  pl.BlockSpec((1, 1, D), lambda i, j: (l, 0, 0)),
                  pl.BlockSpec((1, mod_rows, D), lambda i, j: (mod_idx(i), 0, 0)),
                  pl.BlockSpec((1, mod_rows, D), lambda i, j: (mod_idx(i), 0, 1)),
                  pl.BlockSpec((1, D, tn), lambda i, j: (l, 0, j))],
        out_specs=pl.BlockSpec((tm, tn), lambda i, j: (i, j)),
        out_shape=jax.ShapeDtypeStruct((M, NP), F32),
        scratch_shapes=[pltpu.VMEM((tm, D), BF16)],
        compiler_params=_cp(("parallel", "arbitrary")), name="inproj")(x, g, mod, mod, w)


def _hgrn_kernel(q_ref, f_ref, i_ref, g_ref, lb_ref, gn_ref, o_ref, st_ref, s_scr, *, n_sub):
    c = HG_SUB
    t = pl.program_id(2)

    @pl.when(t == 0)
    def _():
        s_scr[...] = jnp.zeros_like(s_scr)

    lb = lb_ref[0]
    gn = gn_ref[0]
    row = lax.broadcasted_iota(jnp.int32, (c, HG_DK), 0)

    def body(ci, st):
        r0 = pl.multiple_of(ci * c, c)
        z = f_ref[pl.ds(r0, c), :]
        q = q_ref[pl.ds(r0, c), :]
        v = i_ref[pl.ds(r0, c), :]
        gg = g_ref[pl.ds(r0, c), :]
        f = lb + (1.0 - lb) * _sigmoid(z)
        k = (1.0 - lb) * _sigmoid(-z)
        b = jnp.log(jnp.maximum(f, F_FLOOR))
        s = 1
        while s < c:
            b = b + jnp.where(row >= s, pltpu.roll(b, s, axis=0), 0.0)
            s *= 2
        bl = b[c - 1:c]
        o = _dot_nt((q * jnp.exp(b)).astype(BF16), st.astype(BF16))
        for s in range(c):
            w = jnp.where(row >= s, jnp.exp(b - b[s:s + 1]), 0.0) * (q * k[s:s + 1])
            o = o + jnp.sum(w, axis=-1, keepdims=True) * v[s:s + 1]
        ke = k * jnp.exp(bl - b)
        st = st * jnp.exp(bl) + _dot_tn(v.astype(BF16), ke.astype(BF16))
        o_ref[pl.ds(r0, c), :] = _rms(o, gn) * _silu(gg)
        return st

    st = lax.fori_loop(0, n_sub, body, s_scr[...])
    s_scr[...] = st

    @pl.when(t == pl.num_programs(2) - 1)
    def _():
        st_ref[0, 0] = st.T


def _hgrn_prompt(proj, lb, gn, l, B, T):
    tc = min(256, T)
    nt = T // tc
    W = HG_DK

    def col(k):
        return pl.BlockSpec((tc, W), lambda b, h, t: (b * nt + t, k * HG_HEADS + h))

    return pl.pallas_call(
        functools.partial(_hgrn_kernel, n_sub=tc // HG_SUB),
        grid=(B, HG_HEADS, nt),
        in_specs=[col(0), col(1), col(2), col(3),
                  pl.BlockSpec((1, 1, W), lambda b, h, t: (l, 0, h)),
                  pl.BlockSpec((1, 1, W), lambda b, h, t: (l, 0, 0))],
        out_specs=[pl.BlockSpec((tc, W), lambda b, h, t: (b * nt + t, h)),
                   pl.BlockSpec((1, 1, HG_DK, HG_DV), lambda b, h, t: (b, h, 0, 0))],
        out_shape=[jax.ShapeDtypeStruct((B * T, HG_HEADS * HG_DV), F32),
                   jax.ShapeDtypeStruct((B, HG_HEADS, HG_DK, HG_DV), F32)],
        scratch_shapes=[pltpu.VMEM((HG_DV, HG_DK), F32)],
        compiler_params=_cp(("parallel", "parallel", "arbitrary")),
        name="hgrn_prompt")(proj, proj, proj, proj, lb, gn)


def _hgrn_step_kernel(q_ref, f_ref, i_ref, g_ref, lb_ref, gn_ref, s_ref, o_ref, so_ref):
    lb = lb_ref[0]
    z = f_ref[0]
    q = q_ref[0]
    v = i_ref[0]
    gg = g_ref[0]
    fd = jnp.maximum(lb + (1.0 - lb) * _sigmoid(z), F_FLOOR)
    k = (1.0 - lb) * _sigmoid(-z)
    H = HG_HEADS
    pad = jnp.zeros((LANE - 3 * H, HG_DK), F32)
    cols = jnp.concatenate([q, k, fd, pad], axis=0).T
    outs = []
    for h in range(H):
        qc = cols[:, h:h + 1]
        kc = cols[:, H + h:H + h + 1]
        fc = cols[:, 2 * H + h:2 * H + h + 1]
        s_new = fc * s_ref[0, 0, h] + kc * v[h:h + 1]
        so_ref[0, h] = s_new
        outs.append(jnp.sum(s_new * qc, axis=0, keepdims=True))
    o = jnp.concatenate(outs, axis=0)
    o_ref[0] = _rms(o, gn_ref[0]) * _silu(gg)


def _hgrn_step(proj3, lb3, gn, state, l, NS):
    H, W = HG_HEADS, HG_DK

    def part(k):
        return pl.BlockSpec((1, H, W), lambda s: (s, k, 0))

    return pl.pallas_call(
        _hgrn_step_kernel,
        grid=(NS,),
        in_specs=[part(0), part(1), part(2), part(3),
                  pl.BlockSpec((1, H, W), lambda s: (l, 0, 0)),
                  pl.BlockSpec((1, 1, W), lambda s: (l, 0, 0)),
                  pl.BlockSpec((1, 1, H, HG_DK, HG_DV), lambda s: (l, s, 0, 0, 0))],
        out_specs=[pl.BlockSpec((1, H, HG_DV), lambda s: (s, 0, 0)),
                   pl.BlockSpec((1, H, HG_DK, HG_DV), lambda s: (s, 0, 0, 0))],
        out_shape=[jax.ShapeDtypeStruct((NS, H, HG_DV), F32),
                   jax.ShapeDtypeStruct((NS, H, HG_DK, HG_DV), F32)],
        compiler_params=_cp(("parallel",)), name="hgrn_step")(
            proj3, proj3, proj3, proj3, lb3, gn, state)


def _prep_kernel(dq_ref, dk_ref, dv_ref, cq0_ref, cq1_ref, ckv_ref, kr_ref, cos_ref, sin_ref,
                 gq_ref, gk_ref, gcq_ref, gckv_ref, gn_ref, gr_ref, gkr_ref, wuq_ref, wuk_ref,
                 qda_ref, rda_ref, qlat_ref, qrope_ref, rmla_ref):
    cos2 = cos_ref[...]
    sin2 = sin_ref[...]
    for h in range(DA_HEADS):
        x = dq_ref[:, h * LANE:(h + 1) * LANE]
        qda_ref[:, h * LANE:(h + 1) * LANE] = _rope64(_rms64(x, gq_ref[0]), cos2, sin2)
    rda_ref[:, 0:LANE] = _rope64(_rms64(dk_ref[...], gk_ref[0]), cos2, sin2)
    rda_ref[:, LANE:2 * LANE] = dv_ref[...]
    cq = jnp.concatenate([cq0_ref[...], cq1_ref[...]], axis=1)
    qh = _dot(_rms(cq, gcq_ref[0]).astype(BF16), wuq_ref[0])
    nope = [qh[:, h * LANE:(h + 1) * LANE] for h in range(MLA_HEADS)]
    ra = qh[:, 4 * LANE:5 * LANE]
    rb = qh[:, 5 * LANE:6 * LANE]
    lo = lax.broadcasted_iota(jnp.int32, ra.shape, 1) < 64

    def halves(x):
        x2 = x * x
        return (jnp.sum(jnp.where(lo, x2, 0.0), axis=-1, keepdims=True),
                jnp.sum(jnp.where(lo, 0.0, x2), axis=-1, keepdims=True))

    sr = halves(ra) + halves(rb)
    width = float(MLA_NOPE + MLA_ROPE)
    inv = [lax.rsqrt((jnp.sum(nope[h] * nope[h], axis=-1, keepdims=True) + sr[h]) / width + EPS)
           for h in range(MLA_HEADS)]
    for h in range(MLA_HEADS):
        qn = (nope[h] * inv[h] * gn_ref[0]).astype(BF16)
        qlat_ref[:, h * MLA_KV_RANK:(h + 1) * MLA_KV_RANK] = _dot_nt(
            qn, wuk_ref[0, :, h * LANE:(h + 1) * LANE])
    qrope_ref[:, 0:LANE] = _rope64(ra * jnp.where(lo, inv[0], inv[1]) * gr_ref[0], cos2, sin2)
    qrope_ref[:, LANE:2 * LANE] = _rope64(rb * jnp.where(lo, inv[2], inv[3]) * gr_ref[0], cos2, sin2)
    rmla_ref[:, 0:MLA_KV_RANK] = _rms(ckv_ref[...], gckv_ref[0])
    kr = kr_ref[...]
    ms = jnp.sum(kr * kr, axis=-1, keepdims=True) * (1.0 / MLA_ROPE)
    krr = _rope64(kr * lax.rsqrt(ms + EPS) * gkr_ref[0], cos2, sin2)
    rmla_ref[:, MLA_KV_RANK:MLA_ROW] = krr[:, 0:MLA_ROPE]


def _prep(proj, cos2, sin2, pos_tiles, gains, wuq, wuk, l, tm):
    M = proj.shape[0]
    gq, gk, gcq, gckv, gn, gr, gkr = gains

    def col(width, idx):
        return pl.BlockSpec((tm, width), lambda i: (i, idx))

    def vec(n):
        return pl.BlockSpec((1, 1, n), lambda i: (l, 0, 0))

    tab = pl.BlockSpec((tm, LANE), lambda i: (i % pos_tiles, 0))
    return pl.pallas_call(
        _prep_kernel,
        grid=(M // tm,),
        in_specs=[col(512, 8), col(128, 36), col(128, 37), col(256, 19), col(256, 20),
                  col(256, 21), col(128, 44), tab, tab,
                  vec(128), vec(128), vec(512), vec(256), vec(128), vec(128), vec(128),
                  pl.BlockSpec((1, MLA_Q_RANK, 768), lambda i: (l, 0, 0)),
                  pl.BlockSpec((1, MLA_KV_RANK, 512), lambda i: (l, 0, 0))],
        out_specs=[pl.BlockSpec((tm, 512), lambda i: (i, 0)),
                   pl.BlockSpec((tm, DA_ROW), lambda i: (i, 0)),
                   pl.BlockSpec((tm, MLA_HEADS * MLA_KV_RANK), lambda i: (i, 0)),
                   pl.BlockSpec((tm, MLA_HEADS * MLA_ROPE), lambda i: (i, 0)),
                   pl.BlockSpec((tm, MLA_ROW), lambda i: (i, 0))],
        out_shape=[jax.ShapeDtypeStruct((M, 512), F32),
                   jax.ShapeDtypeStruct((M, DA_ROW), F32),
                   jax.ShapeDtypeStruct((M, MLA_HEADS * MLA_KV_RANK), F32),
                   jax.ShapeDtypeStruct((M, MLA_HEADS * MLA_ROPE), F32),
                   jax.ShapeDtypeStruct((M, MLA_ROW), F32)],
        compiler_params=_cp(("parallel",)), name="prep")(
            proj, proj, proj, proj, proj, proj, proj, cos2, sin2,
            gq, gk, gcq, gckv, gn, gr, gkr, wuq, wuk)


def _lam_of(lam_ref, lam_init):
    lv = lam_ref[0]
    a = jnp.sum(lv[0:1] * lv[1:2], axis=-1, keepdims=True)
    b = jnp.sum(lv[2:3] * lv[3:4], axis=-1, keepdims=True)
    return jnp.exp(a) - jnp.exp(b) + lam_init


def _split_halves(q):
    lo = lax.broadcasted_iota(jnp.int32, q.shape, 1) < 64
    return jnp.concatenate([jnp.where(lo, q, 0.0), jnp.where(lo, 0.0, q)], axis=0)


def _online_update(s, v, m_scr, l_scr, acc_scr):
    m_old = m_scr[...]
    m_new = jnp.maximum(m_old, jnp.max(s, axis=-1, keepdims=True))
    alpha = jnp.exp(m_old - m_new)
    p = jnp.exp(s - m_new)
    l_scr[...] = alpha * l_scr[...] + jnp.sum(p, axis=-1, keepdims=True)
    acc_scr[...] = alpha * acc_scr[...] + _dot(p.astype(BF16), v)
    m_scr[...] = m_new


def _causal_mask(s, tq):
    qpos = lax.broadcasted_iota(jnp.int32, s.shape, 0) % tq
    kpos = lax.broadcasted_iota(jnp.int32, s.shape, 1)
    return jnp.where(kpos <= qpos, s, NEG)


def _da_attn_kernel(q_ref, kv_ref, lam_ref, g_ref, o_ref, m_scr, l_scr, acc_scr, *, tq, lam_init):
    i = pl.program_id(1)
    q = q_ref[...]
    qh = jnp.concatenate([q[:, h * LANE:(h + 1) * LANE] for h in range(DA_HEADS)], axis=0)
    qs = (_split_halves(qh) * (DA_HALF ** -0.5)).astype(BF16)
    m_scr[...] = jnp.full_like(m_scr, NEG)
    l_scr[...] = jnp.zeros_like(l_scr)
    acc_scr[...] = jnp.zeros_like(acc_scr)

    def step(j, masked):
        r0 = pl.multiple_of(j * tq, tq)
        kv = kv_ref[0, pl.ds(r0, tq), :]
        s = _dot_nt(qs, kv[:, 0:LANE].astype(BF16))
        if masked:
            s = _causal_mask(s, tq)
        _online_update(s, kv[:, LANE:2 * LANE].astype(BF16), m_scr, l_scr, acc_scr)

    def full(j, carry):
        step(j, False)
        return carry

    lax.fori_loop(0, i, full, 0)
    step(i, True)
    o = acc_scr[...] / l_scr[...]
    n = DA_HEADS * tq
    od = o[0:n] - _lam_of(lam_ref, lam_init) * o[n:2 * n]
    for h in range(DA_HEADS):
        o_ref[:, h * LANE:(h + 1) * LANE] = _rms(od[h * tq:(h + 1) * tq], g_ref[0]) * (1.0 - lam_init)


def _da_attn(q, row3, lam, g, l, lam_init, B, T):
    tq = min(256, T)
    nq = T // tq
    R = 2 * DA_HEADS * tq
    return pl.pallas_call(
        functools.partial(_da_attn_kernel, tq=tq, lam_init=lam_init),
        grid=(B, nq),
        in_specs=[pl.BlockSpec((tq, 512), lambda b, i: (b * nq + i, 0)),
                  pl.BlockSpec((1, T, DA_ROW), lambda b, i: (b, 0, 0)),
                  pl.BlockSpec((1, 4, DA_HALF), lambda b, i: (l, 0, 0)),
                  pl.BlockSpec((1, 1, DA_DV), lambda b, i: (l, 0, 0))],
        out_specs=pl.BlockSpec((tq, DA_HEADS * DA_DV), lambda b, i: (b * nq + i, 0)),
        out_shape=jax.ShapeDtypeStruct((B * T, DA_HEADS * DA_DV), F32),
        scratch_shapes=[pltpu.VMEM((R, 1), F32), pltpu.VMEM((R, 1), F32), pltpu.VMEM((R, DA_DV), F32)],
        compiler_params=_cp(("parallel", "arbitrary")), name="da_attn")(q, row3, lam, g)


def _mla_attn_kernel(ql_ref, qr_ref, kv_ref, wuv_ref, o_ref, m_scr, l_scr, acc_scr, *, tq):
    i = pl.program_id(1)
    ql = ql_ref[...]
    qr = qr_ref[...]
    R = MLA_KV_RANK
    qls = jnp.concatenate([ql[:, h * R:(h + 1) * R] for h in range(MLA_HEADS)], axis=0).astype(BF16)
    qrs = jnp.concatenate([qr[:, h * MLA_ROPE:(h + 1) * MLA_ROPE] for h in range(MLA_HEADS)],
                          axis=0).astype(BF16)
    scale = (MLA_NOPE + MLA_ROPE) ** -0.5
    m_scr[...] = jnp.full_like(m_scr, NEG)
    l_scr[...] = jnp.zeros_like(l_scr)
    acc_scr[...] = jnp.zeros_like(acc_scr)

    def step(j, masked):
        r0 = pl.multiple_of(j * tq, tq)
        kv = kv_ref[0, pl.ds(r0, tq), :]
        lat = kv[:, 0:R].astype(BF16)
        s = (_dot_nt(qls, lat) + _dot_nt(qrs, kv[:, R:MLA_ROW].astype(BF16))) * scale
        if masked:
            s = _causal_mask(s, tq)
        _online_update(s, lat, m_scr, l_scr, acc_scr)

    def full(j, carry):
        step(j, False)
        return carry

    lax.fori_loop(0, i, full, 0)
    step(i, True)
    o = (acc_scr[...] / l_scr[...]).astype(BF16)
    for h in range(MLA_HEADS):
        o_ref[:, h * LANE:(h + 1) * LANE] = _dot(o[h * tq:(h + 1) * tq],
                                                 wuv_ref[0, :, h * LANE:(h + 1) * LANE])


def _mla_attn(ql, qr, row3, wuv, l, B, T):
    tq = min(256, T)
    nq = T // tq
    R = MLA_HEADS * tq
    return pl.pallas_call(
        functools.partial(_mla_attn_kernel, tq=tq),
        grid=(B, nq),
        in_specs=[pl.BlockSpec((tq, MLA_HEADS * MLA_KV_RANK), lambda b, i: (b * nq + i, 0)),
                  pl.BlockSpec((tq, MLA_HEADS * MLA_ROPE), lambda b, i: (b * nq + i, 0)),
                  pl.BlockSpec((1, T, MLA_ROW), lambda b, i: (b, 0, 0)),
                  pl.BlockSpec((1, MLA_KV_RANK, MLA_HEADS * MLA_DV), lambda b, i: (l, 0, 0))],
        out_specs=pl.BlockSpec((tq, MLA_HEADS * MLA_DV), lambda b, i: (b * nq + i, 0)),
        out_shape=jax.ShapeDtypeStruct((B * T, MLA_HEADS * MLA_DV), F32),
        scratch_shapes=[pltpu.VMEM((R, 1), F32), pltpu.VMEM((R, 1), F32),
                        pltpu.VMEM((R, MLA_KV_RANK), F32)],
        compiler_params=_cp(("parallel", "arbitrary")), name="mla_attn")(ql, qr, row3, wuv)


def _da_dec_kernel(pt_ref, q_ref, kn_ref, lam_ref, g_ref, *rest, n_pg, lam_init):
    pages = rest[:n_pg]
    o_ref, m_scr, l_scr, acc_scr = rest[n_pg:]
    c = pl.program_id(1)

    @pl.when(c == 0)
    def _():
        m_scr[...] = jnp.full_like(m_scr, NEG)
        l_scr[...] = jnp.zeros_like(l_scr)
        acc_scr[...] = jnp.zeros_like(acc_scr)

    qf = _split_halves(q_ref[0]) * (DA_HALF ** -0.5)
    qs = qf.astype(BF16)
    s = jnp.concatenate([_dot_nt(qs, pg[0, 0, :, 0:LANE].astype(BF16)) for pg in pages], axis=1)
    m_old = m_scr[...]
    m_new = jnp.maximum(m_old, jnp.max(s, axis=-1, keepdims=True))
    alpha = jnp.exp(m_old - m_new)
    p = jnp.exp(s - m_new)
    l_scr[...] = alpha * l_scr[...] + jnp.sum(p, axis=-1, keepdims=True)
    p = p.astype(BF16)
    pv = None
    for n, pg in enumerate(pages):
        d = _dot(p[:, n * PAGE:(n + 1) * PAGE], pg[0, 0, :, LANE:2 * LANE].astype(BF16))
        pv = d if pv is None else pv + d
    acc_scr[...] = alpha * acc_scr[...] + pv
    m_scr[...] = m_new

    @pl.when(c == pl.num_programs(1) - 1)
    def _():
        kn = kn_ref[0]
        sn = jnp.sum(qf * kn[:, 0:LANE], axis=-1, keepdims=True)
        m2 = jnp.maximum(m_new, sn)
        a2 = jnp.exp(m_new - m2)
        pn = jnp.exp(sn - m2)
        o = (a2 * acc_scr[...] + pn * kn[:, LANE:2 * LANE]) / (a2 * l_scr[...] + pn)
        od = o[0:DA_HEADS] - _lam_of(lam_ref, lam_init) * o[DA_HEADS:2 * DA_HEADS]
        o_ref[0] = _rms(od, g_ref[0]) * (1.0 - lam_init)


def _da_decode(pt, q3, kn3, cache, lam, g, l, lam_init, NS, n_pages):
    n_pg = min(16, n_pages)
    nc = n_pages // n_pg

    def page_spec(n):
        return pl.BlockSpec((1, 1, PAGE, DA_ROW),
                            lambda s, c, pt: (l, pt[s * n_pages + c * n_pg + n], 0, 0))

    gs = pltpu.PrefetchScalarGridSpec(
        num_scalar_prefetch=1, grid=(NS, nc),
        in_specs=[pl.BlockSpec((1, DA_HEADS, LANE), lambda s, c, pt: (s, 0, 0)),
                  pl.BlockSpec((1, 1, DA_ROW), lambda s, c, pt: (s, 0, 0)),
                  pl.BlockSpec((1, 4, DA_HALF), lambda s, c, pt: (l, 0, 0)),
                  pl.BlockSpec((1, 1, DA_DV), lambda s, c, pt: (l, 0, 0))]
        + [page_spec(n) for n in range(n_pg)],
        out_specs=pl.BlockSpec((1, DA_HEADS, DA_DV), lambda s, c, pt: (s, 0, 0)),
        scratch_shapes=[pltpu.VMEM((8, 1), F32), pltpu.VMEM((8, 1), F32), pltpu.VMEM((8, DA_DV), F32)])
    return pl.pallas_call(
        functools.partial(_da_dec_kernel, n_pg=n_pg, lam_init=lam_init),
        grid_spec=gs, out_shape=jax.ShapeDtypeStruct((NS, DA_HEADS, DA_DV), F32),
        compiler_params=_cp(("parallel", "arbitrary")), name="da_decode")(
            pt, q3, kn3, lam, g, *([cache] * n_pg))


def _mla_dec_kernel(pt_ref, ql_ref, qr_ref, kn_ref, *rest, n_pg):
    pages = rest[:n_pg]
    o_ref, m_scr, l_scr, acc_scr = rest[n_pg:]
    c = pl.program_id(1)
    R = MLA_KV_RANK

    @pl.when(c == 0)
    def _():
        m_scr[...] = jnp.full_like(m_scr, NEG)
        l_scr[...] = jnp.zeros_like(l_scr)
        acc_scr[...] = jnp.zeros_like(acc_scr)

    zpad = jnp.zeros((8 - MLA_HEADS, R), F32)
    qlf = jnp.concatenate([ql_ref[0], zpad], axis=0)
    qrf = jnp.concatenate([qr_ref[0], zpad[:, 0:MLA_ROPE]], axis=0)
    qls = qlf.astype(BF16)
    qrs = qrf.astype(BF16)
    scale = (MLA_NOPE + MLA_ROPE) ** -0.5
    s = jnp.concatenate(
        [_dot_nt(qls, pg[0, 0, :, 0:R].astype(BF16)) + _dot_nt(qrs, pg[0, 0, :, R:MLA_ROW].astype(BF16))
         for pg in pages], axis=1) * scale
    m_old = m_scr[...]
    m_new = jnp.maximum(m_old, jnp.max(s, axis=-1, keepdims=True))
    alpha = jnp.exp(m_old - m_new)
    p = jnp.exp(s - m_new)
    l_scr[...] = alpha * l_scr[...] + jnp.sum(p, axis=-1, keepdims=True)
    p = p.astype(BF16)
    pv = None
    for n, pg in enumerate(pages):
        d = _dot(p[:, n * PAGE:(n + 1) * PAGE], pg[0, 0, :, 0:R].astype(BF16))
        pv = d if pv is None else pv + d
    acc_scr[...] = alpha * acc_scr[...] + pv
    m_scr[...] = m_new

    @pl.when(c == pl.num_programs(1) - 1)
    def _():
        kn = kn_ref[0]
        sn = (jnp.sum(qlf * kn[:, 0:R], axis=-1, keepdims=True)
              + jnp.sum(qrf * kn[:, R:MLA_ROW], axis=-1, keepdims=True)) * scale
        m2 = jnp.maximum(m_new, sn)
        a2 = jnp.exp(m_new - m2)
        pn = jnp.exp(sn - m2)
        o = (a2 * acc_scr[...] + pn * kn[:, 0:R]) / (a2 * l_scr[...] + pn)
        o_ref[0] = o[0:MLA_HEADS]


def _mla_decode(pt, ql3, qr3, kn3, cache, l, NS, n_pages):
    n_pg = min(16, n_pages)
    nc = n_pages // n_pg

    def page_spec(n):
        return pl.BlockSpec((1, 1, PAGE, MLA_ROW),
                            lambda s, c, pt: (l, pt[s * n_pages + c * n_pg + n], 0, 0))

    gs = pltpu.PrefetchScalarGridSpec(
        num_scalar_prefetch=1, grid=(NS, nc),
        in_specs=[pl.BlockSpec((1, MLA_HEADS, MLA_KV_RANK), lambda s, c, pt: (s, 0, 0)),
                  pl.BlockSpec((1, MLA_HEADS, MLA_ROPE), lambda s, c, pt: (s, 0, 0)),
                  pl.BlockSpec((1, 1, MLA_ROW), lambda s, c, pt: (s, 0, 0))]
        + [page_spec(n) for n in range(n_pg)],
        out_specs=pl.BlockSpec((1, MLA_HEADS, MLA_KV_RANK), lambda s, c, pt: (s, 0, 0)),
        scratch_shapes=[pltpu.VMEM((8, 1), F32), pltpu.VMEM((8, 1), F32),
                        pltpu.VMEM((8, MLA_KV_RANK), F32)])
    return pl.pallas_call(
        functools.partial(_mla_dec_kernel, n_pg=n_pg),
        grid_spec=gs, out_shape=jax.ShapeDtypeStruct((NS, MLA_HEADS, MLA_KV_RANK), F32),
        compiler_params=_cp(("parallel", "arbitrary")), name="mla_decode")(
            pt, ql3, qr3, kn3, *([cache] * n_pg))


def _uv_kernel(o_ref, w_ref, y_ref):
    o = o_ref[...].astype(BF16)
    R = MLA_KV_RANK
    for h in range(MLA_HEADS):
        y_ref[:, h * LANE:(h + 1) * LANE] = _dot(o[:, h * R:(h + 1) * R],
                                                 w_ref[0, :, h * LANE:(h + 1) * LANE])


def _uv_sample(o_lat2, wuv, l):
    NS = o_lat2.shape[0]
    return pl.pallas_call(
        _uv_kernel, grid=(1,),
        in_specs=[pl.BlockSpec((NS, MLA_HEADS * MLA_KV_RANK), lambda i: (0, 0)),
                  pl.BlockSpec((1, MLA_KV_RANK, MLA_HEADS * MLA_DV), lambda i: (l, 0, 0))],
        out_specs=pl.BlockSpec((NS, MLA_HEADS * MLA_DV), lambda i: (0, 0)),
        out_shape=jax.ShapeDtypeStruct((NS, MLA_HEADS * MLA_DV), F32),
        compiler_params=_cp(("arbitrary",)), name="mla_uv")(o_lat2, wuv)


def _outproj_kernel(oa_ref, ob_ref, oc_ref, wa_ref, wb_ref, wc_ref, x_ref, gate_ref, o_ref):
    acc = (_dot(oa_ref[...].astype(BF16), wa_ref[0]) + _dot(ob_ref[...].astype(BF16), wb_ref[0])
           + _dot(oc_ref[...].astype(BF16), wc_ref[0]))
    o_ref[...] = x_ref[...] + gate_ref[0] * acc


def _outproj(oa, ob, oc, w, x, mod, mod_idx, mod_rows, l, tm):
    M = x.shape[0]
    D = D_MODEL
    tn = 512
    nj = D // tn
    return pl.pallas_call(
        _outproj_kernel,
        grid=(M // tm, nj),
        in_specs=[pl.BlockSpec((tm, 1024), lambda i, j: (i, 0)),
                  pl.BlockSpec((tm, 512), lambda i, j: (i, 0)),
                  pl.BlockSpec((tm, 512), lambda i, j: (i, 0)),
                  pl.BlockSpec((1, 1024, tn), lambda i, j: (l, 0, j)),
                  pl.BlockSpec((1, 512, tn), lambda i, j: (l, 2, j)),
                  pl.BlockSpec((1, 512, tn), lambda i, j: (l, 3, j)),
                  pl.BlockSpec((tm, tn), lambda i, j: (i, j)),
                  pl.BlockSpec((1, mod_rows, tn), lambda i, j: (mod_idx(i), 0, 2 * nj + j))],
        out_specs=pl.BlockSpec((tm, tn), lambda i, j: (i, j)),
        out_shape=jax.ShapeDtypeStruct((M, D), F32),
        compiler_params=_cp(("parallel", "parallel")), name="outproj")(oa, ob, oc, w, w, w, x, mod)


def _ffn_up(x_ref, g_ref, sh_ref, sc_ref, wa_ref, wl_ref, h_scr, acc_scr):
    @pl.when(pl.program_id(1) == 0)
    def _():
        h = _rms(x_ref[...], g_ref[0]) * (1.0 + sc_ref[0]) + sh_ref[0]
        h_scr[...] = h.astype(BF16)
        acc_scr[...] = jnp.zeros_like(acc_scr)

    h = h_scr[...]
    return _dot(h, wa_ref[0]), _dot(h, wl_ref[0])


def _ffn_down(a, am1, am2, lin, cw_ref, cb_ref, wd_ref, x_ref, gate_ref, o_ref, acc_scr):
    cw = cw_ref[0]
    conv = cb_ref[0] + am2 * cw[0:1] + am1 * cw[1:2] + a * cw[2:3]
    act = _silu(conv) * lin
    acc_scr[...] += _dot(act.astype(BF16), wd_ref[0])

    @pl.when(pl.program_id(1) == pl.num_programs(1) - 1)
    def _():
        o_ref[...] = x_ref[...] + gate_ref[0] * acc_scr[...]


def _ffn_prompt_kernel(x_ref, g_ref, sh_ref, sc_ref, gate_ref, wa_ref, wl_ref, cw_ref, cb_ref, wd_ref,
                       o_ref, tail_ref, h_scr, acc_scr, carry_scr, *, tiles_per_seq):
    i = pl.program_id(0)
    f = pl.program_id(1)
    a, lin = _ffn_up(x_ref, g_ref, sh_ref, sc_ref, wa_ref, wl_ref, h_scr, acc_scr)
    tm = a.shape[0]

    @pl.when(i % tiles_per_seq == 0)
    def _():
        carry_scr[f] = jnp.zeros(carry_scr.shape[1:], F32)

    prev = carry_scr[f]
    row = lax.broadcasted_iota(jnp.int32, a.shape, 0)
    am1 = jnp.where(row == 0, prev[7:8], pltpu.roll(a, 1, axis=0))
    am2 = jnp.where(row == 0, prev[6:7], jnp.where(row == 1, prev[7:8], pltpu.roll(a, 2, axis=0)))
    last = a[tm - 8:tm]
    carry_scr[f] = last
    tail_ref[0] = last
    _ffn_down(a, am1, am2, lin, cw_ref, cb_ref, wd_ref, x_ref, gate_ref, o_ref, acc_scr)


def _ffn_sample_kernel(x_ref, g_ref, sh_ref, sc_ref, gate_ref, wa_ref, wl_ref, cw_ref, cb_ref, wd_ref,
                       st_ref, o_ref, tail_ref, h_scr, acc_scr):
    a, lin = _ffn_up(x_ref, g_ref, sh_ref, sc_ref, wa_ref, wl_ref, h_scr, acc_scr)
    am2 = st_ref[0, 0]
    am1 = st_ref[0, 1]
    tail_ref[0] = am1
    tail_ref[1] = a
    _ffn_down(a, am1, am2, lin, cw_ref, cb_ref, wd_ref, x_ref, gate_ref, o_ref, acc_scr)


def _ffn(x, g, mod, mod_idx, mod_rows, w_up, conv_w, conv_b, w_down, l, tm, seq_len=None, state=None):
    M = x.shape[0]
    D = D_MODEL
    tf = 512
    nf = D_FF // tf
    in_specs = [pl.BlockSpec((tm, D), lambda i, f: (i, 0)),
                pl.BlockSpec((1, 1, D), lambda i, f: (l, 0, 0)),
                pl.BlockSpec((1, mod_rows, D), lambda i, f: (mod_idx(i), 0, 3)),
                pl.BlockSpec((1, mod_rows, D), lambda i, f: (mod_idx(i), 0, 4)),
                pl.BlockSpec((1, mod_rows, D), lambda i, f: (mod_idx(i), 0, 5)),
                pl.BlockSpec((1, D, tf), lambda i, f: (l, 0, f)),
                pl.BlockSpec((1, D, tf), lambda i, f: (l, 0, nf + f)),
                pl.BlockSpec((1, 3, tf), lambda i, f: (l, 0, f)),
                pl.BlockSpec((1, 1, tf), lambda i, f: (l, 0, f)),
                pl.BlockSpec((1, tf, D), lambda i, f: (l, f, 0))]
    args = [x, g, mod, mod, mod, w_up, w_up, conv_w, conv_b, w_down]
    scratch = [pltpu.VMEM((tm, D), BF16), pltpu.VMEM((tm, D), F32)]
    if state is None:
        kern = functools.partial(_ffn_prompt_kernel, tiles_per_seq=seq_len // tm)
        tail_spec = pl.BlockSpec((1, 8, tf), lambda i, f: (i, 0, f))
        tail_shape = jax.ShapeDtypeStruct((M // tm, 8, D_FF), F32)
        scratch = scratch + [pltpu.VMEM((nf, 8, tf), F32)]
    else:
        kern = _ffn_sample_kernel
        in_specs.append(pl.BlockSpec((1, 2, tm, tf), lambda i, f: (l, 0, 0, f)))
        args.append(state)
        tail_spec = pl.BlockSpec((2, tm, tf), lambda i, f: (0, 0, f))
        tail_shape = jax.ShapeDtypeStruct((2, M, D_FF), F32)
    return pl.pallas_call(
        kern, grid=(M // tm, nf), in_specs=in_specs,
        out_specs=[pl.BlockSpec((tm, D), lambda i, f: (i, 0)), tail_spec],
        out_shape=[jax.ShapeDtypeStruct((M, D), F32), tail_shape],
        scratch_shapes=scratch,
        compiler_params=_cp(("arbitrary", "arbitrary")), name="convffn")(*args)


def _rope_tables(pos):
    inv = ROPE_THETA ** (-jnp.arange(0, 64, 2, dtype=F32) / 64)
    ang = pos.astype(F32)[:, None] * inv[None, :]
    cos, sin = jnp.cos(ang), jnp.sin(ang)
    cos2 = jnp.tile(cos, (1, 4))
    sin2 = jnp.tile(jnp.concatenate([-sin, sin], axis=1), (1, 2))
    return cos2, sin2


def kernel(x_prompt, x_sample, cache_diff_kv, cache_mla_latent, state_hgrn, state_ffn_conv, page_table, c_prompt, c_sample, norm_mix_g, norm_ffn_g, w_ada, b_ada, w_in, hg_lb_logits, hg_norm_g, da_lam, da_qnorm_g, da_knorm_g, da_onorm_g, mla_cq_norm_g, mla_ckv_norm_g, mla_w_uq, mla_w_uk, mla_w_uv, mla_qnorm_g, mla_knorm_g, w_out, w_up, conv_w, conv_b, w_down):
    B, T, D = x_prompt.shape
    NS = x_sample.shape[0]
    L = w_in.shape[0]
    n_pages = page_table.shape[1]
    past_len = n_pages * PAGE
    assert x_sample.shape[1] == 1 and D == D_MODEL

    w_in_b = jnp.pad(w_in.astype(BF16), ((0, 0), (0, 0), (0, PROJ_PAD - PROJ_WIDTH)))
    w_out_b = w_out.astype(BF16)
    w_up_b = w_up.astype(BF16)
    w_down_b = w_down.astype(BF16)
    wq4 = mla_w_uq.reshape(L, MLA_Q_RANK, MLA_HEADS, MLA_NOPE + MLA_ROPE)
    wuq_b = jnp.concatenate([wq4[..., :MLA_NOPE].reshape(L, MLA_Q_RANK, -1),
                             wq4[..., MLA_NOPE:].reshape(L, MLA_Q_RANK, -1)], axis=-1).astype(BF16)
    wuk_b = mla_w_uk.astype(BF16)
    wuv_b = mla_w_uv.astype(BF16)

    def vec(a):
        return a.reshape(L, 1, -1)

    gains = (vec(jnp.tile(da_qnorm_g, (1, 2))), vec(jnp.tile(da_knorm_g, (1, 2))),
             vec(mla_cq_norm_g), vec(mla_ckv_norm_g), vec(mla_qnorm_g[:, :MLA_NOPE]),
             vec(jnp.tile(mla_qnorm_g[:, MLA_NOPE:], (1, 2))),
             vec(jnp.pad(mla_knorm_g, ((0, 0), (0, LANE - MLA_ROPE)))))
    g_mix, g_ffn = vec(norm_mix_g), vec(norm_ffn_g)
    g_hg, g_da = vec(hg_norm_g), vec(da_onorm_g)
    conv_b3 = vec(conv_b)
    conv_state = state_ffn_conv.transpose(0, 2, 1, 3)

    lb = _lower_bounds(hg_lb_logits)
    lb_p = lb.reshape(L, 1, HG_HEADS * HG_DK)
    lb_s = lb.reshape(L, HG_HEADS, HG_DK)

    n_rows = -(-(NS + B) // 8) * 8
    c_all = jnp.concatenate([c_sample, c_prompt, jnp.zeros((n_rows - NS - B, D), F32)], axis=0)
    mod_s = _ada(c_all, w_ada, vec(b_ada))
    mod_p = mod_s[:, NS:NS + B].reshape(L * B, 1, 6 * D)

    cos_p, sin_p = _rope_tables(jnp.arange(T))
    cos_s, sin_s = _rope_tables(jnp.full((NS,), past_len))
    pt_flat = page_table.reshape(-1)

    tm_p = min(1024, T)
    tm_ffn = min(512, T)
    tm_prep = min(256, T)
    xp = x_prompt.reshape(B * T, D)
    xs = x_sample.reshape(NS, D)
    outs = [[] for _ in range(8)]
    for l in range(L):
        lam_init = 0.8 - 0.6 * math.exp(-0.3 * l)

        def idx_p(tm):
            return lambda i: l * B + (i * tm) // T

        def idx_s(i):
            return l

        proj = _inproj(xp, g_mix, mod_p, idx_p(tm_p), 1, w_in_b, l, tm_p)
        o_a, st_p = _hgrn_prompt(proj, lb_p, g_hg, l, B, T)
        q_da, row_da, q_lat, q_rope, row_mla = _prep(
            proj, cos_p, sin_p, T // tm_prep, gains, wuq_b, wuk_b, l, tm_prep)
        o_b = _da_attn(q_da, row_da.reshape(B, T, DA_ROW), da_lam, g_da, l, lam_init, B, T)
        o_c = _mla_attn(q_lat, q_rope, row_mla.reshape(B, T, MLA_ROW), wuv_b, l, B, T)
        xp = _outproj(o_a, o_b, o_c, w_out_b, xp, mod_p, idx_p(tm_p), 1, l, tm_p)
        xp, tail_p = _ffn(xp, g_ffn, mod_p, idx_p(tm_ffn), 1, w_up_b, conv_w, conv_b3, w_down_b, l,
                          tm_ffn, seq_len=T)
        tps = T // tm_ffn
        cv_p = tail_p.reshape(B, tps, 8, D_FF)[:, tps - 1, 6:8]

        proj_s = _inproj(xs, g_mix, mod_s, idx_s, NS, w_in_b, l, NS)
        oa_s, st_s = _hgrn_step(proj_s.reshape(NS, PROJ_PAD // LANE, LANE), lb_s, g_hg, state_hgrn, l, NS)
        qda_s, rda_s, qlat_s, qrope_s, rmla_s = _prep(
            proj_s, cos_s, sin_s, 1, gains, wuq_b, wuk_b, l, NS)
        ob_s = _da_decode(pt_flat, qda_s.reshape(NS, DA_HEADS, LANE), rda_s.reshape(NS, 1, DA_ROW),
                          cache_diff_kv, da_lam, g_da, l, lam_init, NS, n_pages)
        olat_s = _mla_decode(pt_flat, qlat_s.reshape(NS, MLA_HEADS, MLA_KV_RANK),
                             qrope_s.reshape(NS, MLA_HEADS, MLA_ROPE), rmla_s.reshape(NS, 1, MLA_ROW),
                             cache_mla_latent, l, NS, n_pages)
        oc_s = _uv_sample(olat_s.reshape(NS, MLA_HEADS * MLA_KV_RANK), wuv_b, l)
        xs = _outproj(oa_s.reshape(NS, HG_HEADS * HG_DV), ob_s.reshape(NS, DA_HEADS * DA_DV), oc_s,
                      w_out_b, xs, mod_s, idx_s, NS, l, NS)
        xs, tail_s = _ffn(xs, g_ffn, mod_s, idx_s, NS, w_up_b, conv_w, conv_b3, w_down_b, l, NS,
                          state=conv_state)

        for lst, val in zip(outs, (row_da.reshape(B, T, DA_ROW), rda_s.reshape(NS, 1, DA_ROW),
                                   row_mla.reshape(B, T, MLA_ROW), rmla_s.reshape(NS, 1, MLA_ROW),
                                   st_p, st_s, cv_p, tail_s.transpose(1, 0, 2))):
            lst.append(val)

    return (xp.reshape(B, T, D), xs.reshape(NS, 1, D)) + tuple(jnp.stack(o) for o in outs)
```

```python
import functools
import math

import jax
import jax.numpy as jnp
from jax import lax
from jax.experimental import pallas as pl
from jax.experimental.pallas import tpu as pltpu

F32 = jnp.float32
BF16 = jnp.bfloat16

D_MODEL = 2048
HG_HEADS = 8
HG_DK = 128
HG_DV = 128
DA_HEADS = 4
DA_HALF = 64
DA_DV = 128
DA_ROW = 256
MLA_HEADS = 4
MLA_Q_RANK = 512
MLA_KV_RANK = 256
MLA_NOPE = 128
MLA_ROPE = 64
MLA_DV = 128
MLA_ROW = MLA_KV_RANK + MLA_ROPE
D_FF = 5632
PAGE = 128
ROPE_THETA = 10000.0
EPS = 1e-6
NEG = -1e30
F_FLOOR = 1e-30
LOG2E = 1.4426950408889634
PROJ_WIDTH = 5696
PROJ_PAD = 6144
LANE = 128
HG_SUB = 16
DEC_PAGES = 64
VMEM_LIMIT = 48 * 1024 * 1024


def _cp(sem, vmem=VMEM_LIMIT):
    return pltpu.CompilerParams(dimension_semantics=sem, vmem_limit_bytes=vmem)


def _sigmoid(x):
    return 1.0 / (1.0 + jnp.exp(-x))


def _silu(x):
    return x * _sigmoid(x)


def _rms(x, g):
    ms = jnp.mean(x * x, axis=-1, keepdims=True)
    return x * lax.rsqrt(ms + EPS) * g


def _rms64(x, g2):
    lo = lax.broadcasted_iota(jnp.int32, x.shape, 1) < 64
    x2 = x * x
    s_lo = jnp.sum(jnp.where(lo, x2, 0.0), axis=-1, keepdims=True)
    s_hi = jnp.sum(jnp.where(lo, 0.0, x2), axis=-1, keepdims=True)
    ms = jnp.where(lo, s_lo, s_hi) * (1.0 / 64.0)
    return x * lax.rsqrt(ms + EPS) * g2


def _rope64(x, cos2, sin2):
    lane = lax.broadcasted_iota(jnp.int32, x.shape, 1)
    first = (lane % 64) < 32
    sw = jnp.where(first, pltpu.roll(x, LANE - 32, axis=1), pltpu.roll(x, 32, axis=1))
    return x * cos2 + sw * sin2


def _dot(a, b):
    return jnp.dot(a, b, preferred_element_type=F32)


def _dot_nt(a, b):
    return lax.dot_general(a, b, (((1,), (1,)), ((), ())), preferred_element_type=F32)


def _dot_tn(a, b):
    return lax.dot_general(a, b, (((0,), (0,)), ((), ())), preferred_element_type=F32)


def _lb_kernel(x_ref, o_ref):
    x = x_ref[...]
    n = x.shape[0]
    m = jnp.max(x, axis=0, keepdims=True)
    e = jnp.exp(x - m)
    p = e / jnp.sum(e, axis=0, keepdims=True)
    rows = []
    c = None
    for l in range(n):
        c = p[l:l + 1] if c is None else c + p[l:l + 1]
        rows.append(c - p[0:1])
    o_ref[...] = jnp.concatenate(rows, axis=0)


def _lower_bounds(logits):
    return pl.pallas_call(
        _lb_kernel, out_shape=jax.ShapeDtypeStruct(logits.shape, F32), name="lower_bounds")(logits)


def _ada_kernel(c_ref, w_ref, b_ref, o_ref):
    a = _silu(c_ref[...]).astype(BF16)
    o_ref[0] = _dot(a, w_ref[0].astype(BF16)) + b_ref[0]


def _ada(c_all, w_ada, b_ada):
    L, _, N = w_ada.shape
    R = c_all.shape[0]
    tn = 1024
    return pl.pallas_call(
        _ada_kernel,
        grid=(L, N // tn),
        in_specs=[pl.BlockSpec((R, D_MODEL), lambda l, j: (0, 0)),
                  pl.BlockSpec((1, D_MODEL, tn), lambda l, j: (l, 0, j)),
                  pl.BlockSpec((1, 1, tn), lambda l, j: (l, 0, j))],
        out_specs=pl.BlockSpec((1, R, tn), lambda l, j: (l, 0, j)),
        out_shape=jax.ShapeDtypeStruct((L, R, N), F32),
        compiler_params=_cp(("parallel", "parallel")), name="adaln")(c_all, w_ada, b_ada)


def _inproj_kernel(x_ref, g_ref, sh_ref, sc_ref, w_ref, o_ref, h_scr):
    @pl.when(pl.program_id(1) == 0)
    def _():
        h = _rms(x_ref[...], g_ref[0]) * (1.0 + sc_ref[0]) + sh_ref[0]
        h_scr[...] = h.astype(BF16)

    o_ref[...] = _dot(h_scr[...], w_ref[0])


def _inproj(x, g, mod, mod_idx, mod_rows, w, l, tm):
    M = x.shape[0]
    NP = w.shape[2]
    tn = 512
    D = D_MODEL
    return pl.pallas_call(
        _inproj_kernel,
        grid=(M // tm, NP // tn),
        in_specs=[pl.BlockSpec((tm, D), lambda i, j: (i, 0)),
                  pl.BlockSpec((1, 1, D), lambda i, j: (l, 0, 0)),
                  pl.BlockSpec((1, mod_rows, D), lambda i, j: (mod_idx(i), 0, 0)),
                  pl.BlockSpec((1, mod_rows, D), lambda i, j: (mod_idx(i), 0, 1)),
                  pl.BlockSpec((1, D, tn), lambda i, j: (l, 0, j))],
        out_specs=pl.BlockSpec((tm, tn), lambda i, j: (i, j)),
        out_shape=jax.ShapeDtypeStruct((M, NP), F32),
        scratch_shapes=[pltpu.VMEM((tm, D), BF16)],
        compiler_params=_cp(("parallel", "arbitrary")), name="inproj")(x, g, mod, mod, w)


def _hgrn_kernel(q_ref, f_ref, i_ref, g_ref, lb_ref, gn_ref, o_ref, st_ref, s_scr, *, n_sub):
    c = HG_SUB
    t = pl.program_id(1)

    @pl.when(t == 0)
    def _():
        s_scr[...] = jnp.zeros_like(s_scr)

    gn = gn_ref[0]
    row = lax.broadcasted_iota(jnp.int32, (c, HG_DK), 0)
    ones = jnp.ones((HG_DK, LANE), BF16)

    def body(ci, carry):
        r0 = pl.multiple_of(ci * c, c)
        rows = pl.ds(r0, c)
        for h in range(HG_HEADS):
            cols = slice(h * HG_DK, (h + 1) * HG_DK)
            lb = lb_ref[0, :, cols]
            z = f_ref[rows, cols]
            q = q_ref[rows, cols]
            v = i_ref[rows, cols]
            f = lb + (1.0 - lb) * _sigmoid(z)
            k = (1.0 - lb) * _sigmoid(-z)
            b = jnp.log(jnp.maximum(f, F_FLOOR))
            s = 1
            while s < c:
                b = b + jnp.where(row >= s, pltpu.roll(b, s, axis=0), 0.0)
                s *= 2
            b = b * LOG2E
            bl = b[c - 1:c]
            st = s_scr[h]
            o = _dot_nt((q * jnp.exp2(b)).astype(BF16), st.astype(BF16))
            w = jnp.concatenate(
                [(jnp.where(row >= s, jnp.exp2(b - b[s:s + 1]), 0.0) * (q * k[s:s + 1])).astype(BF16)
                 for s in range(c)], axis=0)
            a = _dot(w, ones)
            for s in range(c):
                o = o + a[s * c:(s + 1) * c] * v[s:s + 1]
            ke = k * jnp.exp2(bl - b)
            s_scr[h] = st * jnp.exp2(bl) + _dot_tn(v.astype(BF16), ke.astype(BF16))
            o_ref[rows, cols] = _rms(o, gn) * _silu(g_ref[rows, cols])
        return carry

    lax.fori_loop(0, n_sub, body, 0)

    @pl.when(t == pl.num_programs(1) - 1)
    def _():
        for h in range(HG_HEADS):
            st_ref[0, h] = s_scr[h].T


def _hgrn_prompt(proj, lb, gn, l, B, T):
    tc = min(256, T)
    nt = T // tc
    W = HG_HEADS * HG_DK

    def col(k):
        return pl.BlockSpec((tc, W), lambda b, t: (b * nt + t, k))

    return pl.pallas_call(
        functools.partial(_hgrn_kernel, n_sub=tc // HG_SUB),
        grid=(B, nt),
        in_specs=[col(0), col(1), col(2), col(3),
                  pl.BlockSpec((1, 1, W), lambda b, t: (l, 0, 0)),
                  pl.BlockSpec((1, 1, HG_DV), lambda b, t: (l, 0, 0))],
        out_specs=[pl.BlockSpec((tc, W), lambda b, t: (b * nt + t, 0)),
                   pl.BlockSpec((1, HG_HEADS, HG_DK, HG_DV), lambda b, t: (b, 0, 0, 0))],
        out_shape=[jax.ShapeDtypeStruct((B * T, HG_HEADS * HG_DV), F32),
                   jax.ShapeDtypeStruct((B, HG_HEADS, HG_DK, HG_DV), F32)],
        scratch_shapes=[pltpu.VMEM((HG_HEADS, HG_DV, HG_DK), F32)],
        compiler_params=_cp(("parallel", "arbitrary")),
        name="hgrn_prompt")(proj, proj, proj, proj, lb, gn)


def _hgrn_step_kernel(q_ref, f_ref, i_ref, g_ref, lb_ref, gn_ref, s_ref, o_ref, so_ref):
    lb = lb_ref[0]
    H = HG_HEADS
    pad = jnp.zeros((LANE - 3 * H, HG_DK), F32)
    for j in range(q_ref.shape[0]):
        z = f_ref[j]
        q = q_ref[j]
        v = i_ref[j]
        fd = jnp.maximum(lb + (1.0 - lb) * _sigmoid(z), F_FLOOR)
        k = (1.0 - lb) * _sigmoid(-z)
        cols = jnp.concatenate([q, k, fd, pad], axis=0).T
        outs = []
        for h in range(H):
            qc = cols[:, h:h + 1]
            kc = cols[:, H + h:H + h + 1]
            fc = cols[:, 2 * H + h:2 * H + h + 1]
            s_new = fc * s_ref[0, j, h] + kc * v[h:h + 1]
            so_ref[j, h] = s_new
            outs.append(jnp.sum(s_new * qc, axis=0, keepdims=True))
        o = jnp.concatenate(outs, axis=0)
        o_ref[j] = _rms(o, gn_ref[0]) * _silu(g_ref[j])


def _hgrn_step(proj3, lb3, gn, state, l, NS):
    H, W = HG_HEADS, HG_DK
    sb = 4 if NS % 4 == 0 else 1

    def part(k):
        return pl.BlockSpec((sb, H, W), lambda s: (s, k, 0))

    return pl.pallas_call(
        _hgrn_step_kernel,
        grid=(NS // sb,),
        in_specs=[part(0), part(1), part(2), part(3),
                  pl.BlockSpec((1, H, W), lambda s: (l, 0, 0)),
                  pl.BlockSpec((1, 1, W), lambda s: (l, 0, 0)),
                  pl.BlockSpec((1, sb, H, HG_DK, HG_DV), lambda s: (l, s, 0, 0, 0))],
        out_specs=[pl.BlockSpec((sb, H, HG_DV), lambda s: (s, 0, 0)),
                   pl.BlockSpec((sb, H, HG_DK, HG_DV), lambda s: (s, 0, 0, 0))],
        out_shape=[jax.ShapeDtypeStruct((NS, H, HG_DV), F32),
                   jax.ShapeDtypeStruct((NS, H, HG_DK, HG_DV), F32)],
        compiler_params=_cp(("parallel",)), name="hgrn_step")(
            proj3, proj3, proj3, proj3, lb3, gn, state)


def _prep_kernel(dq_ref, dk_ref, dv_ref, cq0_ref, cq1_ref, ckv_ref, kr_ref, cos_ref, sin_ref,
                 gq_ref, gk_ref, gcq_ref, gckv_ref, gn_ref, gr_ref, gkr_ref, wuq_ref, wuk_ref,
                 qda_ref, rda_ref, qlat_ref, qrope_ref, rmla_ref):
    cos2 = cos_ref[...]
    sin2 = sin_ref[...]
    for h in range(DA_HEADS):
        x = dq_ref[:, h * LANE:(h + 1) * LANE]
        qda_ref[:, h * LANE:(h + 1) * LANE] = _rope64(_rms64(x, gq_ref[0]), cos2, sin2)
    rda_ref[:, 0:LANE] = _rope64(_rms64(dk_ref[...], gk_ref[0]), cos2, sin2)
    rda_ref[:, LANE:2 * LANE] = dv_ref[...]
    cq = jnp.concatenate([cq0_ref[...], cq1_ref[...]], axis=1)
    qh = _dot(_rms(cq, gcq_ref[0]).astype(BF16), wuq_ref[0])
    nope = [qh[:, h * LANE:(h + 1) * LANE] for h in range(MLA_HEADS)]
    ra = qh[:, 4 * LANE:5 * LANE]
    rb = qh[:, 5 * LANE:6 * LANE]
    lo = lax.broadcasted_iota(jnp.int32, ra.shape, 1) < 64

    def halves(x):
        x2 = x * x
        return (jnp.sum(jnp.where(lo, x2, 0.0), axis=-1, keepdims=True),
                jnp.sum(jnp.where(lo, 0.0, x2), axis=-1, keepdims=True))

    sr = halves(ra) + halves(rb)
    width = float(MLA_NOPE + MLA_ROPE)
    inv = [lax.rsqrt((jnp.sum(nope[h] * nope[h], axis=-1, keepdims=True) + sr[h]) / width + EPS)
           for h in range(MLA_HEADS)]
    for h in range(MLA_HEADS):
        qn = (nope[h] * inv[h] * gn_ref[0]).astype(BF16)
        qlat_ref[:, h * MLA_KV_RANK:(h + 1) * MLA_KV_RANK] = _dot_nt(
            qn, wuk_ref[0, :, h * LANE:(h + 1) * LANE])
    qrope_ref[:, 0:LANE] = _rope64(ra * jnp.where(lo, inv[0], inv[1]) * gr_ref[0], cos2, sin2)
    qrope_ref[:, LANE:2 * LANE] = _rope64(rb * jnp.where(lo, inv[2], inv[3]) * gr_ref[0], cos2, sin2)
    rmla_ref[:, 0:MLA_KV_RANK] = _rms(ckv_ref[...], gckv_ref[0])
    kr = kr_ref[...]
    ms = jnp.sum(kr * kr, axis=-1, keepdims=True) * (1.0 / MLA_ROPE)
    krr = _rope64(kr * lax.rsqrt(ms + EPS) * gkr_ref[0], cos2, sin2)
    rmla_ref[:, MLA_KV_RANK:MLA_ROW] = krr[:, 0:MLA_ROPE]


def _prep(proj, cos2, sin2, pos_tiles, gains, wuq, wuk, l, tm):
    M = proj.shape[0]
    gq, gk, gcq, gckv, gn, gr, gkr = gains

    def col(width, idx):
        return pl.BlockSpec((tm, width), lambda i: (i, idx))

    def vec(n):
        return pl.BlockSpec((1, 1, n), lambda i: (l, 0, 0))

    tab = pl.BlockSpec((tm, LANE), lambda i: (i % pos_tiles, 0))
    return pl.pallas_call(
        _prep_kernel,
        grid=(M // tm,),
        in_specs=[col(512, 8), col(128, 36), col(128, 37), col(256, 19), col(256, 20),
                  col(256, 21), col(128, 44), tab, tab,
                  vec(128), vec(128), vec(512), vec(256), vec(128), vec(128), vec(128),
                  pl.BlockSpec((1, MLA_Q_RANK, 768), lambda i: (l, 0, 0)),
                  pl.BlockSpec((1, MLA_KV_RANK, 512), lambda i: (l, 0, 0))],
        out_specs=[pl.BlockSpec((tm, 512), lambda i: (i, 0)),
                   pl.BlockSpec((tm, DA_ROW), lambda i: (i, 0)),
                   pl.BlockSpec((tm, MLA_HEADS * MLA_KV_RANK), lambda i: (i, 0)),
                   pl.BlockSpec((tm, MLA_HEADS * MLA_ROPE), lambda i: (i, 0)),
                   pl.BlockSpec((tm, MLA_ROW), lambda i: (i, 0))],
        out_shape=[jax.ShapeDtypeStruct((M, 512), F32),
                   jax.ShapeDtypeStruct((M, DA_ROW), F32),
                   jax.ShapeDtypeStruct((M, MLA_HEADS * MLA_KV_RANK), F32),
                   jax.ShapeDtypeStruct((M, MLA_HEADS * MLA_ROPE), F32),
                   jax.ShapeDtypeStruct((M, MLA_ROW), F32)],
        compiler_params=_cp(("parallel",)), name="prep")(
            proj, proj, proj, proj, proj, proj, proj, cos2, sin2,
            gq, gk, gcq, gckv, gn, gr, gkr, wuq, wuk)


def _lam_of(lam_ref, lam_init):
    lv = lam_ref[0]
    a = jnp.sum(lv[0:1] * lv[1:2], axis=-1, keepdims=True)
    b = jnp.sum(lv[2:3] * lv[3:4], axis=-1, keepdims=True)
    return jnp.exp(a) - jnp.exp(b) + lam_init


def _split_halves(q):
    lo = lax.broadcasted_iota(jnp.int32, q.shape, 1) < 64
    return jnp.concatenate([jnp.where(lo, q, 0.0), jnp.where(lo, 0.0, q)], axis=0)


def _lanes(x, width):
    return x if width == LANE else pltpu.repeat(x, width // LANE, axis=1)


def _online_update(s, v, m_scr, l_scr, acc_scr):
    m_old = m_scr[...]
    m_new = jnp.maximum(m_old, jnp.max(s, axis=-1, keepdims=True))
    alpha = jnp.exp2(m_old - m_new)
    p = jnp.exp2(s - _lanes(m_new, s.shape[1]))
    l_scr[...] = alpha * l_scr[...] + jnp.sum(p, axis=-1, keepdims=True)
    acc_scr[...] = _lanes(alpha, acc_scr.shape[1]) * acc_scr[...] + _dot(p.astype(BF16), v)
    m_scr[...] = m_new


def _causal_mask(s, tq, q0, k0):
    qpos = q0 + lax.broadcasted_iota(jnp.int32, s.shape, 0) % tq
    kpos = k0 + lax.broadcasted_iota(jnp.int32, s.shape, 1)
    return jnp.where(kpos <= qpos, s, NEG)


def _causal_sweep(i, tq, step):
    tk = 2 * tq
    n_full = i // 2

    def full(j, carry):
        step(pl.multiple_of(j * tk, tk), False)
        return carry

    lax.fori_loop(0, n_full, full, 0)
    step(pl.multiple_of(n_full * tk, tk), True)


def _da_attn_kernel(q_ref, kv_ref, lam_ref, g_ref, o_ref, m_scr, l_scr, acc_scr, *, tq, lam_init):
    i = pl.program_id(1)
    q = q_ref[...]
    qh = jnp.concatenate([q[:, h * LANE:(h + 1) * LANE] for h in range(DA_HEADS)], axis=0)
    qs = (_split_halves(qh) * (DA_HALF ** -0.5 * LOG2E)).astype(BF16)
    m_scr[...] = jnp.full_like(m_scr, NEG)
    l_scr[...] = jnp.zeros_like(l_scr)
    acc_scr[...] = jnp.zeros_like(acc_scr)

    def step(r0, masked):
        kv = kv_ref[0, pl.ds(r0, 2 * tq), :]
        s = _dot_nt(qs, kv[:, 0:LANE].astype(BF16))
        if masked:
            s = _causal_mask(s, tq, i * tq, r0)
        _online_update(s, kv[:, LANE:2 * LANE].astype(BF16), m_scr, l_scr, acc_scr)

    _causal_sweep(i, tq, step)
    o = acc_scr[...] / l_scr[...]
    n = DA_HEADS * tq
    od = o[0:n] - _lam_of(lam_ref, lam_init) * o[n:2 * n]
    for h in range(DA_HEADS):
        o_ref[:, h * LANE:(h + 1) * LANE] = _rms(od[h * tq:(h + 1) * tq], g_ref[0]) * (1.0 - lam_init)


def _da_attn(q, row3, lam, g, l, lam_init, B, T):
    tq = min(256, T // 2)
    nq = T // tq
    R = 2 * DA_HEADS * tq
    return pl.pallas_call(
        functools.partial(_da_attn_kernel, tq=tq, lam_init=lam_init),
        grid=(B, nq),
        in_specs=[pl.BlockSpec((tq, 512), lambda b, i: (b * nq + i, 0)),
                  pl.BlockSpec((1, T, DA_ROW), lambda b, i: (b, 0, 0)),
                  pl.BlockSpec((1, 4, DA_HALF), lambda b, i: (l, 0, 0)),
                  pl.BlockSpec((1, 1, DA_DV), lambda b, i: (l, 0, 0))],
        out_specs=pl.BlockSpec((tq, DA_HEADS * DA_DV), lambda b, i: (b * nq + i, 0)),
        out_shape=jax.ShapeDtypeStruct((B * T, DA_HEADS * DA_DV), F32),
        scratch_shapes=[pltpu.VMEM((R, LANE), F32), pltpu.VMEM((R, LANE), F32),
                        pltpu.VMEM((R, DA_DV), F32)],
        compiler_params=_cp(("parallel", "arbitrary")), name="da_attn")(q, row3, lam, g)


def _mla_attn_kernel(ql_ref, qr_ref, kv_ref, wuv_ref, o_ref, m_scr, l_scr, acc_scr, *, tq):
    i = pl.program_id(1)
    ql = ql_ref[...]
    qr = qr_ref[...]
    R = MLA_KV_RANK
    scale = (MLA_NOPE + MLA_ROPE) ** -0.5 * LOG2E
    qls = (jnp.concatenate([ql[:, h * R:(h + 1) * R] for h in range(MLA_HEADS)], axis=0)
           * scale).astype(BF16)
    qrs = (jnp.concatenate([qr[:, h * MLA_ROPE:(h + 1) * MLA_ROPE] for h in range(MLA_HEADS)], axis=0)
           * scale).astype(BF16)
    m_scr[...] = jnp.full_like(m_scr, NEG)
    l_scr[...] = jnp.zeros_like(l_scr)
    acc_scr[...] = jnp.zeros_like(acc_scr)

    def step(r0, masked):
        kv = kv_ref[0, pl.ds(r0, 2 * tq), :]
        lat = kv[:, 0:R].astype(BF16)
        s = _dot_nt(qls, lat) + _dot_nt(qrs, kv[:, R:MLA_ROW].astype(BF16))
        if masked:
            s = _causal_mask(s, tq, i * tq, r0)
        _online_update(s, lat, m_scr, l_scr, acc_scr)

    _causal_sweep(i, tq, step)
    o = (acc_scr[...] / _lanes(l_scr[...], R)).astype(BF16)
    for h in range(MLA_HEADS):
        o_ref[:, h * LANE:(h + 1) * LANE] = _dot(o[h * tq:(h + 1) * tq],
                                                 wuv_ref[0, :, h * LANE:(h + 1) * LANE])


def _mla_attn(ql, qr, row3, wuv, l, B, T):
    tq = min(256, T // 2)
    nq = T // tq
    R = MLA_HEADS * tq
    return pl.pallas_call(
        functools.partial(_mla_attn_kernel, tq=tq),
        grid=(B, nq),
        in_specs=[pl.BlockSpec((tq, MLA_HEADS * MLA_KV_RANK), lambda b, i: (b * nq + i, 0)),
                  pl.BlockSpec((tq, MLA_HEADS * MLA_ROPE), lambda b, i: (b * nq + i, 0)),
                  pl.BlockSpec((1, T, MLA_ROW), lambda b, i: (b, 0, 0)),
                  pl.BlockSpec((1, MLA_KV_RANK, MLA_HEADS * MLA_DV), lambda b, i: (l, 0, 0))],
        out_specs=pl.BlockSpec((tq, MLA_HEADS * MLA_DV), lambda b, i: (b * nq + i, 0)),
        out_shape=jax.ShapeDtypeStruct((B * T, MLA_HEADS * MLA_DV), F32),
        scratch_shapes=[pltpu.VMEM((R, LANE), F32), pltpu.VMEM((R, LANE), F32),
                        pltpu.VMEM((R, MLA_KV_RANK), F32)],
        compiler_params=_cp(("parallel", "arbitrary")), name="mla_attn")(ql, qr, row3, wuv)


def _da_dec_kernel(pt_ref, q_ref, kn_ref, lam_ref, g_ref, *rest, n_pg, lam_init):
    pages = rest[:n_pg]
    o_ref, m_scr, l_scr, acc_scr = rest[n_pg:]
    c = pl.program_id(1)

    @pl.when(c == 0)
    def _():
        m_scr[...] = jnp.full_like(m_scr, NEG)
        l_scr[...] = jnp.zeros_like(l_scr)
        acc_scr[...] = jnp.zeros_like(acc_scr)

    qf = _split_halves(q_ref[0]) * (DA_HALF ** -0.5)
    qs = qf.astype(BF16)
    s = jnp.concatenate([_dot_nt(qs, pg[0, 0, :, 0:LANE].astype(BF16)) for pg in pages], axis=1)
    m_old = m_scr[...]
    m_new = jnp.maximum(m_old, jnp.max(s, axis=-1, keepdims=True))
    alpha = jnp.exp(m_old - m_new)
    p = jnp.exp(s - m_new)
    l_scr[...] = alpha * l_scr[...] + jnp.sum(p, axis=-1, keepdims=True)
    p = p.astype(BF16)
    pv = None
    for n, pg in enumerate(pages):
        d = _dot(p[:, n * PAGE:(n + 1) * PAGE], pg[0, 0, :, LANE:2 * LANE].astype(BF16))
        pv = d if pv is None else pv + d
    acc_scr[...] = alpha * acc_scr[...] + pv
    m_scr[...] = m_new

    @pl.when(c == pl.num_programs(1) - 1)
    def _():
        kn = kn_ref[0]
        sn = jnp.sum(qf * kn[:, 0:LANE], axis=-1, keepdims=True)
        m2 = jnp.maximum(m_new, sn)
        a2 = jnp.exp(m_new - m2)
        pn = jnp.exp(sn - m2)
        o = (a2 * acc_scr[...] + pn * kn[:, LANE:2 * LANE]) / (a2 * l_scr[...] + pn)
        od = o[0:DA_HEADS] - _lam_of(lam_ref, lam_init) * o[DA_HEADS:2 * DA_HEADS]
        o_ref[0] = _rms(od, g_ref[0]) * (1.0 - lam_init)


def _da_decode(pt, q3, kn3, cache, lam, g, l, lam_init, NS, n_pages):
    n_pg = min(DEC_PAGES, n_pages)
    nc = n_pages // n_pg

    def page_spec(n):
        return pl.BlockSpec((1, 1, PAGE, DA_ROW),
                            lambda s, c, pt: (l, pt[s * n_pages + c * n_pg + n], 0, 0))

    gs = pltpu.PrefetchScalarGridSpec(
        num_scalar_prefetch=1, grid=(NS, nc),
        in_specs=[pl.BlockSpec((1, DA_HEADS, LANE), lambda s, c, pt: (s, 0, 0)),
                  pl.BlockSpec((1, 1, DA_ROW), lambda s, c, pt: (s, 0, 0)),
                  pl.BlockSpec((1, 4, DA_HALF), lambda s, c, pt: (l, 0, 0)),
                  pl.BlockSpec((1, 1, DA_DV), lambda s, c, pt: (l, 0, 0))]
        + [page_spec(n) for n in range(n_pg)],
        out_specs=pl.BlockSpec((1, DA_HEADS, DA_DV), lambda s, c, pt: (s, 0, 0)),
        scratch_shapes=[pltpu.VMEM((8, 1), F32), pltpu.VMEM((8, 1), F32), pltpu.VMEM((8, DA_DV), F32)])
    return pl.pallas_call(
        functools.partial(_da_dec_kernel, n_pg=n_pg, lam_init=lam_init),
        grid_spec=gs, out_shape=jax.ShapeDtypeStruct((NS, DA_HEADS, DA_DV), F32),
        compiler_params=_cp(("parallel", "arbitrary")), name="da_decode")(
            pt, q3, kn3, lam, g, *([cache] * n_pg))


def _mla_dec_kernel(pt_ref, ql_ref, qr_ref, kn_ref, *rest, n_pg):
    pages = rest[:n_pg]
    o_ref, m_scr, l_scr, acc_scr = rest[n_pg:]
    c = pl.program_id(1)
    R = MLA_KV_RANK

    @pl.when(c == 0)
    def _():
        m_scr[...] = jnp.full_like(m_scr, NEG)
        l_scr[...] = jnp.zeros_like(l_scr)
        acc_scr[...] = jnp.zeros_like(acc_scr)

    zpad = jnp.zeros((8 - MLA_HEADS, R), F32)
    qlf = jnp.concatenate([ql_ref[0], zpad], axis=0)
    qrf = jnp.concatenate([qr_ref[0], zpad[:, 0:MLA_ROPE]], axis=0)
    qs = jnp.concatenate([qlf, qrf], axis=1).astype(BF16)
    scale = (MLA_NOPE + MLA_ROPE) ** -0.5
    s = jnp.concatenate([_dot(qs, pg[0, 0].astype(BF16)) for pg in pages], axis=1) * scale
    m_old = m_scr[...]
    m_new = jnp.maximum(m_old, jnp.max(s, axis=-1, keepdims=True))
    alpha = jnp.exp(m_old - m_new)
    p = jnp.exp(s - m_new)
    l_scr[...] = alpha * l_scr[...] + jnp.sum(p, axis=-1, keepdims=True)
    p = p.astype(BF16)
    pv = None
    for n, pg in enumerate(pages):
        d = _dot_nt(p[:, n * PAGE:(n + 1) * PAGE], pg[0, 0, 0:R, :].astype(BF16))
        pv = d if pv is None else pv + d
    acc_scr[...] = alpha * acc_scr[...] + pv
    m_scr[...] = m_new

    @pl.when(c == pl.num_programs(1) - 1)
    def _():
        kn = kn_ref[0]
        sn = (jnp.sum(qlf * kn[:, 0:R], axis=-1, keepdims=True)
              + jnp.sum(qrf * kn[:, R:MLA_ROW], axis=-1, keepdims=True)) * scale
        m2 = jnp.maximum(m_new, sn)
        a2 = jnp.exp(m_new - m2)
        pn = jnp.exp(sn - m2)
        o = (a2 * acc_scr[...] + pn * kn[:, 0:R]) / (a2 * l_scr[...] + pn)
        o_ref[0] = o[0:MLA_HEADS]


def _mla_decode(pt, ql3, qr3, kn3, cache_t, l, NS, n_pages):
    n_pg = min(DEC_PAGES, n_pages)
    nc = n_pages // n_pg

    def page_spec(n):
        return pl.BlockSpec((1, 1, MLA_ROW, PAGE),
                            lambda s, c, pt: (l, pt[s * n_pages + c * n_pg + n], 0, 0))

    gs = pltpu.PrefetchScalarGridSpec(
        num_scalar_prefetch=1, grid=(NS, nc),
        in_specs=[pl.BlockSpec((1, MLA_HEADS, MLA_KV_RANK), lambda s, c, pt: (s, 0, 0)),
                  pl.BlockSpec((1, MLA_HEADS, MLA_ROPE), lambda s, c, pt: (s, 0, 0)),
                  pl.BlockSpec((1, 1, MLA_ROW), lambda s, c, pt: (s, 0, 0))]
        + [page_spec(n) for n in range(n_pg)],
        out_specs=pl.BlockSpec((1, MLA_HEADS, MLA_KV_RANK), lambda s, c, pt: (s, 0, 0)),
        scratch_shapes=[pltpu.VMEM((8, 1), F32), pltpu.VMEM((8, 1), F32),
                        pltpu.VMEM((8, MLA_KV_RANK), F32)])
    return pl.pallas_call(
        functools.partial(_mla_dec_kernel, n_pg=n_pg),
        grid_spec=gs, out_shape=jax.ShapeDtypeStruct((NS, MLA_HEADS, MLA_KV_RANK), F32),
        compiler_params=_cp(("parallel", "arbitrary")), name="mla_decode")(
            pt, ql3, qr3, kn3, *([cache_t] * n_pg))


def _uv_kernel(o_ref, w_ref, y_ref):
    o = o_ref[...].astype(BF16)
    R = MLA_KV_RANK
    for h in range(MLA_HEADS):
        y_ref[:, h * LANE:(h + 1) * LANE] = _dot(o[:, h * R:(h + 1) * R],
                                                 w_ref[0, :, h * LANE:(h + 1) * LANE])


def _uv_sample(o_lat2, wuv, l):
    NS = o_lat2.shape[0]
    return pl.pallas_call(
        _uv_kernel, grid=(1,),
        in_specs=[pl.BlockSpec((NS, MLA_HEADS * MLA_KV_RANK), lambda i: (0, 0)),
                  pl.BlockSpec((1, MLA_KV_RANK, MLA_HEADS * MLA_DV), lambda i: (l, 0, 0))],
        out_specs=pl.BlockSpec((NS, MLA_HEADS * MLA_DV), lambda i: (0, 0)),
        out_shape=jax.ShapeDtypeStruct((NS, MLA_HEADS * MLA_DV), F32),
        compiler_params=_cp(("arbitrary",)), name="mla_uv")(o_lat2, wuv)


def _outproj_kernel(oa_ref, ob_ref, oc_ref, wa_ref, wb_ref, wc_ref, x_ref, gate_ref, o_ref):
    acc = (_dot(oa_ref[...].astype(BF16), wa_ref[0]) + _dot(ob_ref[...].astype(BF16), wb_ref[0])
           + _dot(oc_ref[...].astype(BF16), wc_ref[0]))
    o_ref[...] = x_ref[...] + gate_ref[0] * acc


def _outproj(oa, ob, oc, w, x, mod, mod_idx, mod_rows, l, tm):
    M = x.shape[0]
    D = D_MODEL
    tn = 512
    nj = D // tn
    return pl.pallas_call(
        _outproj_kernel,
        grid=(M // tm, nj),
        in_specs=[pl.BlockSpec((tm, 1024), lambda i, j: (i, 0)),
                  pl.BlockSpec((tm, 512), lambda i, j: (i, 0)),
                  pl.BlockSpec((tm, 512), lambda i, j: (i, 0)),
                  pl.BlockSpec((1, 1024, tn), lambda i, j: (l, 0, j)),
                  pl.BlockSpec((1, 512, tn), lambda i, j: (l, 2, j)),
                  pl.BlockSpec((1, 512, tn), lambda i, j: (l, 3, j)),
                  pl.BlockSpec((tm, tn), lambda i, j: (i, j)),
                  pl.BlockSpec((1, mod_rows, tn), lambda i, j: (mod_idx(i), 0, 2 * nj + j))],
        out_specs=pl.BlockSpec((tm, tn), lambda i, j: (i, j)),
        out_shape=jax.ShapeDtypeStruct((M, D), F32),
        compiler_params=_cp(("parallel", "parallel")), name="outproj")(oa, ob, oc, w, w, w, x, mod)


def _ffn_up(x_ref, g_ref, sh_ref, sc_ref, wa_ref, wl_ref, h_scr, acc_scr):
    @pl.when(pl.program_id(1) == 0)
    def _():
        h = _rms(x_ref[...], g_ref[0]) * (1.0 + sc_ref[0]) + sh_ref[0]
        h_scr[...] = h.astype(BF16)
        acc_scr[...] = jnp.zeros_like(acc_scr)

    h = h_scr[...]
    return _dot(h, wa_ref[0]), _dot(h, wl_ref[0])


def _ffn_down(a, am1, am2, lin, cw_ref, cb_ref, wd_ref, x_ref, gate_ref, o_ref, acc_scr):
    cw = cw_ref[0]
    conv = cb_ref[0] + am2 * cw[0:1] + am1 * cw[1:2] + a * cw[2:3]
    act = _silu(conv) * lin
    acc_scr[...] += _dot(act.astype(BF16), wd_ref[0])

    @pl.when(pl.program_id(1) == pl.num_programs(1) - 1)
    def _():
        o_ref[...] = x_ref[...] + gate_ref[0] * acc_scr[...]


def _ffn_prompt_kernel(x_ref, g_ref, sh_ref, sc_ref, gate_ref, wa_ref, wl_ref, cw_ref, cb_ref, wd_ref,
                       o_ref, tail_ref, h_scr, acc_scr, carry_scr, *, tiles_per_seq):
    i = pl.program_id(0)
    f = pl.program_id(1)
    a, lin = _ffn_up(x_ref, g_ref, sh_ref, sc_ref, wa_ref, wl_ref, h_scr, acc_scr)
    tm = a.shape[0]

    @pl.when(i % tiles_per_seq == 0)
    def _():
        carry_scr[f] = jnp.zeros(carry_scr.shape[1:], F32)

    prev = carry_scr[f]
    row = lax.broadcasted_iota(jnp.int32, a.shape, 0)
    am1 = jnp.where(row == 0, prev[7:8], pltpu.roll(a, 1, axis=0))
    am2 = jnp.where(row == 0, prev[6:7], jnp.where(row == 1, prev[7:8], pltpu.roll(a, 2, axis=0)))
    last = a[tm - 8:tm]
    carry_scr[f] = last
    tail_ref[0] = last
    _ffn_down(a, am1, am2, lin, cw_ref, cb_ref, wd_ref, x_ref, gate_ref, o_ref, acc_scr)


def _ffn_sample_kernel(x_ref, g_ref, sh_ref, sc_ref, gate_ref, wa_ref, wl_ref, cw_ref, cb_ref, wd_ref,
                       st_ref, o_ref, tail_ref, h_scr, acc_scr):
    a, lin = _ffn_up(x_ref, g_ref, sh_ref, sc_ref, wa_ref, wl_ref, h_scr, acc_scr)
    am2 = st_ref[0, 0]
    am1 = st_ref[0, 1]
    tail_ref[0] = am1
    tail_ref[1] = a
    _ffn_down(a, am1, am2, lin, cw_ref, cb_ref, wd_ref, x_ref, gate_ref, o_ref, acc_scr)


def _ffn(x, g, mod, mod_idx, mod_rows, w_up, conv_w, conv_b, w_down, l, tm, seq_len=None, state=None):
    M = x.shape[0]
    D = D_MODEL
    tf = 512
    nf = D_FF // tf
    in_specs = [pl.BlockSpec((tm, D), lambda i, f: (i, 0)),
                pl.BlockSpec((1, 1, D), lambda i, f: (l, 0, 0)),
                pl.BlockSpec((1, mod_rows, D), lambda i, f: (mod_idx(i), 0, 3)),
                pl.BlockSpec((1, mod_rows, D), lambda i, f: (mod_idx(i), 0, 4)),
                pl.BlockSpec((1, mod_rows, D), lambda i, f: (mod_idx(i), 0, 5)),
                pl.BlockSpec((1, D, tf), lambda i, f: (l, 0, f)),
                pl.BlockSpec((1, D, tf), lambda i, f: (l, 0, nf + f)),
                pl.BlockSpec((1, 3, tf), lambda i, f: (l, 0, f)),
                pl.BlockSpec((1, 1, tf), lambda i, f: (l, 0, f)),
                pl.BlockSpec((1, tf, D), lambda i, f: (l, f, 0))]
    args = [x, g, mod, mod, mod, w_up, w_up, conv_w, conv_b, w_down]
    scratch = [pltpu.VMEM((tm, D), BF16), pltpu.VMEM((tm, D), F32)]
    if state is None:
        kern = functools.partial(_ffn_prompt_kernel, tiles_per_seq=seq_len // tm)
        tail_spec = pl.BlockSpec((1, 8, tf), lambda i, f: (i, 0, f))
        tail_shape = jax.ShapeDtypeStruct((M // tm, 8, D_FF), F32)
        scratch = scratch + [pltpu.VMEM((nf, 8, tf), F32)]
    else:
        kern = _ffn_sample_kernel
        in_specs.append(pl.BlockSpec((1, 2, tm, tf), lambda i, f: (l, 0, 0, f)))
        args.append(state)
        tail_spec = pl.BlockSpec((2, tm, tf), lambda i, f: (0, 0, f))
        tail_shape = jax.ShapeDtypeStruct((2, M, D_FF), F32)
    return pl.pallas_call(
        kern, grid=(M // tm, nf), in_specs=in_specs,
        out_specs=[pl.BlockSpec((tm, D), lambda i, f: (i, 0)), tail_spec],
        out_shape=[jax.ShapeDtypeStruct((M, D), F32), tail_shape],
        scratch_shapes=scratch,
        compiler_params=_cp(("arbitrary", "arbitrary")), name="convffn")(*args)


def _rope_tables(pos):
    inv = ROPE_THETA ** (-jnp.arange(0, 64, 2, dtype=F32) / 64)
    ang = pos.astype(F32)[:, None] * inv[None, :]
    cos, sin = jnp.cos(ang), jnp.sin(ang)
    cos2 = jnp.tile(cos, (1, 4))
    sin2 = jnp.tile(jnp.concatenate([-sin, sin], axis=1), (1, 2))
    return cos2, sin2


def kernel(x_prompt, x_sample, cache_diff_kv, cache_mla_latent, state_hgrn, state_ffn_conv, page_table, c_prompt, c_sample, norm_mix_g, norm_ffn_g, w_ada, b_ada, w_in, hg_lb_logits, hg_norm_g, da_lam, da_qnorm_g, da_knorm_g, da_onorm_g, mla_cq_norm_g, mla_ckv_norm_g, mla_w_uq, mla_w_uk, mla_w_uv, mla_qnorm_g, mla_knorm_g, w_out, w_up, conv_w, conv_b, w_down):
    B, T, D = x_prompt.shape
    NS = x_sample.shape[0]
    L = w_in.shape[0]
    n_pages = page_table.shape[1]
    past_len = n_pages * PAGE
    assert x_sample.shape[1] == 1 and D == D_MODEL

    w_in_b = jnp.pad(w_in.astype(BF16), ((0, 0), (0, 0), (0, PROJ_PAD - PROJ_WIDTH)))
    w_out_b = w_out.astype(BF16)
    w_up_b = w_up.astype(BF16)
    w_down_b = w_down.astype(BF16)
    wq4 = mla_w_uq.reshape(L, MLA_Q_RANK, MLA_HEADS, MLA_NOPE + MLA_ROPE)
    wuq_b = jnp.concatenate([wq4[..., :MLA_NOPE].reshape(L, MLA_Q_RANK, -1),
                             wq4[..., MLA_NOPE:].reshape(L, MLA_Q_RANK, -1)], axis=-1).astype(BF16)
    wuk_b = mla_w_uk.astype(BF16)
    wuv_b = mla_w_uv.astype(BF16)

    def vec(a):
        return a.reshape(L, 1, -1)

    gains = (vec(jnp.tile(da_qnorm_g, (1, 2))), vec(jnp.tile(da_knorm_g, (1, 2))),
             vec(mla_cq_norm_g), vec(mla_ckv_norm_g), vec(mla_qnorm_g[:, :MLA_NOPE]),
             vec(jnp.tile(mla_qnorm_g[:, MLA_NOPE:], (1, 2))),
             vec(jnp.pad(mla_knorm_g, ((0, 0), (0, LANE - MLA_ROPE)))))
    g_mix, g_ffn = vec(norm_mix_g), vec(norm_ffn_g)
    g_hg, g_da = vec(hg_norm_g), vec(da_onorm_g)
    conv_b3 = vec(conv_b)
    conv_state = state_ffn_conv.transpose(0, 2, 1, 3)
    cache_mla_t = jnp.swapaxes(cache_mla_latent, 2, 3)

    lb = _lower_bounds(hg_lb_logits)
    lb_p = lb.reshape(L, 1, HG_HEADS * HG_DK)
    lb_s = lb.reshape(L, HG_HEADS, HG_DK)

    n_rows = -(-(NS + B) // 8) * 8
    c_all = jnp.concatenate([c_sample, c_prompt, jnp.zeros((n_rows - NS - B, D), F32)], axis=0)
    mod_s = _ada(c_all, w_ada, vec(b_ada))
    mod_p = mod_s[:, NS:NS + B].reshape(L * B, 1, 6 * D)

    cos_p, sin_p = _rope_tables(jnp.arange(T))
    cos_s, sin_s = _rope_tables(jnp.full((NS,), past_len))
    pt_flat = page_table.reshape(-1)

    tm_p = min(1024, T)
    tm_ffn = min(512, T)
    tm_prep = min(256, T)
    xp = x_prompt.reshape(B * T, D)
    xs = x_sample.reshape(NS, D)
    outs = [[] for _ in range(8)]
    for l in range(L):
        lam_init = 0.8 - 0.6 * math.exp(-0.3 * l)

        def idx_p(tm):
            return lambda i: l * B + (i * tm) // T

        def idx_s(i):
            return l

        proj = _inproj(xp, g_mix, mod_p, idx_p(tm_p), 1, w_in_b, l, tm_p)
        o_a, st_p = _hgrn_prompt(proj, lb_p, g_hg, l, B, T)
        q_da, row_da, q_lat, q_rope, row_mla = _prep(
            proj, cos_p, sin_p, T // tm_prep, gains, wuq_b, wuk_b, l, tm_prep)
        o_b = _da_attn(q_da, row_da.reshape(B, T, DA_ROW), da_lam, g_da, l, lam_init, B, T)
        o_c = _mla_attn(q_lat, q_rope, row_mla.reshape(B, T, MLA_ROW), wuv_b, l, B, T)
        xp = _outproj(o_a, o_b, o_c, w_out_b, xp, mod_p, idx_p(tm_p), 1, l, tm_p)
        xp, tail_p = _ffn(xp, g_ffn, mod_p, idx_p(tm_ffn), 1, w_up_b, conv_w, conv_b3, w_down_b, l,
                          tm_ffn, seq_len=T)
        tps = T // tm_ffn
        cv_p = tail_p.reshape(B, tps, 8, D_FF)[:, tps - 1, 6:8]

        proj_s = _inproj(xs, g_mix, mod_s, idx_s, NS, w_in_b, l, NS)
        oa_s, st_s = _hgrn_step(proj_s.reshape(NS, PROJ_PAD // LANE, LANE), lb_s, g_hg, state_hgrn, l, NS)
        qda_s, rda_s, qlat_s, qrope_s, rmla_s = _prep(
            proj_s, cos_s, sin_s, 1, gains, wuq_b, wuk_b, l, NS)
        ob_s = _da_decode(pt_flat, qda_s.reshape(NS, DA_HEADS, LANE), rda_s.reshape(NS, 1, DA_ROW),
                          cache_diff_kv, da_lam, g_da, l, lam_init, NS, n_pages)
        olat_s = _mla_decode(pt_flat, qlat_s.reshape(NS, MLA_HEADS, MLA_KV_RANK),
                             qrope_s.reshape(NS, MLA_HEADS, MLA_ROPE), rmla_s.reshape(NS, 1, MLA_ROW),
                             cache_mla_t, l, NS, n_pages)
        oc_s = _uv_sample(olat_s.reshape(NS, MLA_HEADS * MLA_KV_RANK), wuv_b, l)
        xs = _outproj(oa_s.reshape(NS, HG_HEADS * HG_DV), ob_s.reshape(NS, DA_HEADS * DA_DV), oc_s,
                      w_out_b, xs, mod_s, idx_s, NS, l, NS)
        xs, tail_s = _ffn(xs, g_ffn, mod_s, idx_s, NS, w_up_b, conv_w, conv_b3, w_down_b, l, NS,
                          state=conv_state)

        for lst, val in zip(outs, (row_da.reshape(B, T, DA_ROW), rda_s.reshape(NS, 1, DA_ROW),
                                   row_mla.reshape(B, T, MLA_ROW), rmla_s.reshape(NS, 1, MLA_ROW),
                                   st_p, st_s, cv_p, tail_s.transpose(1, 0, 2))):
            lst.append(val)

    return (xp.reshape(B, T, D), xs.reshape(NS, 1, D)) + tuple(jnp.stack(o) for o in outs)
```

```python
import functools
import math

import jax
import jax.numpy as jnp
from jax import lax
from jax.experimental import pallas as pl
from jax.experimental.pallas import tpu as pltpu

F32 = jnp.float32
BF16 = jnp.bfloat16

D_MODEL = 2048
HG_HEADS = 8
HG_DK = 128
HG_DV = 128
DA_HEADS = 4
DA_HALF = 64
DA_DV = 128
DA_ROW = 256
MLA_HEADS = 4
MLA_Q_RANK = 512
MLA_KV_RANK = 256
MLA_NOPE = 128
MLA_ROPE = 64
MLA_DV = 128
MLA_ROW = MLA_KV_RANK + MLA_ROPE
D_FF = 5632
PAGE = 128
ROPE_THETA = 10000.0
EPS = 1e-6
NEG = -1e30
F_FLOOR = 1e-30
LOG2E = 1.4426950408889634
PROJ_WIDTH = 5696
PROJ_PAD = 6144
LANE = 128
HG_SUB = 16
DEC_BUF_BYTES = 24 * 1024 * 1024
VMEM_LIMIT = 48 * 1024 * 1024


def _cp(sem, vmem=VMEM_LIMIT):
    return pltpu.CompilerParams(dimension_semantics=sem, vmem_limit_bytes=vmem)


def _sigmoid(x):
    return 1.0 / (1.0 + jnp.exp(-x))


def _silu(x):
    return x * _sigmoid(x)


def _rms(x, g):
    ms = jnp.mean(x * x, axis=-1, keepdims=True)
    return x * lax.rsqrt(ms + EPS) * g


def _rms64(x, g2):
    lo = lax.broadcasted_iota(jnp.int32, x.shape, 1) < 64
    x2 = x * x
    s_lo = jnp.sum(jnp.where(lo, x2, 0.0), axis=-1, keepdims=True)
    s_hi = jnp.sum(jnp.where(lo, 0.0, x2), axis=-1, keepdims=True)
    ms = jnp.where(lo, s_lo, s_hi) * (1.0 / 64.0)
    return x * lax.rsqrt(ms + EPS) * g2


def _rope64(x, cos2, sin2):
    lane = lax.broadcasted_iota(jnp.int32, x.shape, 1)
    first = (lane % 64) < 32
    sw = jnp.where(first, pltpu.roll(x, LANE - 32, axis=1), pltpu.roll(x, 32, axis=1))
    return x * cos2 + sw * sin2


def _dot(a, b):
    return jnp.dot(a, b, preferred_element_type=F32)


def _dot_nt(a, b):
    return lax.dot_general(a, b, (((1,), (1,)), ((), ())), preferred_element_type=F32)


def _dot_tn(a, b):
    return lax.dot_general(a, b, (((0,), (0,)), ((), ())), preferred_element_type=F32)


def _lb_kernel(x_ref, o_ref):
    x = x_ref[...]
    n = x.shape[0]
    m = jnp.max(x, axis=0, keepdims=True)
    e = jnp.exp(x - m)
    p = e / jnp.sum(e, axis=0, keepdims=True)
    rows = []
    c = None
    for l in range(n):
        c = p[l:l + 1] if c is None else c + p[l:l + 1]
        rows.append(c - p[0:1])
    o_ref[...] = jnp.concatenate(rows, axis=0)


def _lower_bounds(logits):
    return pl.pallas_call(
        _lb_kernel, out_shape=jax.ShapeDtypeStruct(logits.shape, F32), name="lower_bounds")(logits)


def _ada_kernel(c_ref, w_ref, b_ref, o_ref):
    a = _silu(c_ref[...]).astype(BF16)
    o_ref[0] = _dot(a, w_ref[0].astype(BF16)) + b_ref[0]


def _ada(c_all, w_ada, b_ada):
    L, _, N = w_ada.shape
    R = c_all.shape[0]
    tn = 1024
    return pl.pallas_call(
        _ada_kernel,
        grid=(L, N // tn),
        in_specs=[pl.BlockSpec((R, D_MODEL), lambda l, j: (0, 0)),
                  pl.BlockSpec((1, D_MODEL, tn), lambda l, j: (l, 0, j)),
                  pl.BlockSpec((1, 1, tn), lambda l, j: (l, 0, j))],
        out_specs=pl.BlockSpec((1, R, tn), lambda l, j: (l, 0, j)),
        out_shape=jax.ShapeDtypeStruct((L, R, N), F32),
        compiler_params=_cp(("parallel", "parallel")), name="adaln")(c_all, w_ada, b_ada)


def _inproj_kernel(x_ref, g_ref, sh_ref, sc_ref, w_ref, o_ref, h_scr):
    @pl.when(pl.program_id(1) == 0)
    def _():
        h = _rms(x_ref[...], g_ref[0]) * (1.0 + sc_ref[0]) + sh_ref[0]
        h_scr[...] = h.astype(BF16)

    o_ref[...] = _dot(h_scr[...], w_ref[0])


def _inproj(x, g, mod, mod_idx, mod_rows, w, l, tm):
    M = x.shape[0]
    NP = w.shape[2]
    tn = 512
    D = D_MODEL
    return pl.pallas_call(
        _inproj_kernel,
        grid=(M // tm, NP // tn),
        in_specs=[pl.BlockSpec((tm, D), lambda i, j: (i, 0)),
                  pl.BlockSpec((1, 1, D), lambda i, j: (l, 0, 0)),
                  pl.BlockSpec((1, mod_rows, D), lambda i, j: (mod_idx(i), 0, 0)),
                  pl.BlockSpec((1, mod_rows, D), lambda i, j: (mod_idx(i), 0, 1)),
                  pl.BlockSpec((1, D, tn), lambda i, j: (l, 0, j))],
        out_specs=pl.BlockSpec((tm, tn), lambda i, j: (i, j)),
        out_shape=jax.ShapeDtypeStruct((M, NP), F32),
        scratch_shapes=[pltpu.VMEM((tm, D), BF16)],
        compiler_params=_cp(("parallel", "arbitrary")), name="inproj")(x, g, mod, mod, w)


def _hgrn_kernel(q_ref, f_ref, i_ref, g_ref, lb_ref, gn_ref, o_ref, st_ref, s_scr, *, n_sub):
    c = HG_SUB
    t = pl.program_id(1)

    @pl.when(t == 0)
    def _():
        s_scr[...] = jnp.zeros_like(s_scr)

    gn = gn_ref[0]
    row = lax.broadcasted_iota(jnp.int32, (c, HG_DK), 0)
    ones = jnp.ones((HG_DK, LANE), BF16)

    def body(ci, carry):
        r0 = pl.multiple_of(ci * c, c)
        rows = pl.ds(r0, c)
        for h in range(HG_HEADS):
            cols = slice(h * HG_DK, (h + 1) * HG_DK)
            lb = lb_ref[0, :, cols]
            z = f_ref[rows, cols]
            q = q_ref[rows, cols]
            v = i_ref[rows, cols]
            f = lb + (1.0 - lb) * _sigmoid(z)
            k = (1.0 - lb) * _sigmoid(-z)
            b = jnp.log(jnp.maximum(f, F_FLOOR))
            s = 1
            while s < c:
                b = b + jnp.where(row >= s, pltpu.roll(b, s, axis=0), 0.0)
                s *= 2
            b = b * LOG2E
            bl = b[c - 1:c]
            st = s_scr[h]
            o = _dot_nt((q * jnp.exp2(b)).astype(BF16), st.astype(BF16))
            w = jnp.concatenate(
                [(jnp.where(row >= s, jnp.exp2(b - b[s:s + 1]), 0.0) * (q * k[s:s + 1])).astype(BF16)
                 for s in range(c)], axis=0)
            a = _dot(w, ones)
            for s in range(c):
                o = o + a[s * c:(s + 1) * c] * v[s:s + 1]
            ke = k * jnp.exp2(bl - b)
            s_scr[h] = st * jnp.exp2(bl) + _dot_tn(v.astype(BF16), ke.astype(BF16))
            o_ref[rows, cols] = _rms(o, gn) * _silu(g_ref[rows, cols])
        return carry

    lax.fori_loop(0, n_sub, body, 0)

    @pl.when(t == pl.num_programs(1) - 1)
    def _():
        for h in range(HG_HEADS):
            st_ref[0, h] = s_scr[h].T


def _hgrn_prompt(proj, lb, gn, l, B, T):
    tc = min(256, T)
    nt = T // tc
    W = HG_HEADS * HG_DK

    def col(k):
        return pl.BlockSpec((tc, W), lambda b, t: (b * nt + t, k))

    return pl.pallas_call(
        functools.partial(_hgrn_kernel, n_sub=tc // HG_SUB),
        grid=(B, nt),
        in_specs=[col(0), col(1), col(2), col(3),
                  pl.BlockSpec((1, 1, W), lambda b, t: (l, 0, 0)),
                  pl.BlockSpec((1, 1, HG_DV), lambda b, t: (l, 0, 0))],
        out_specs=[pl.BlockSpec((tc, W), lambda b, t: (b * nt + t, 0)),
                   pl.BlockSpec((1, HG_HEADS, HG_DK, HG_DV), lambda b, t: (b, 0, 0, 0))],
        out_shape=[jax.ShapeDtypeStruct((B * T, HG_HEADS * HG_DV), F32),
                   jax.ShapeDtypeStruct((B, HG_HEADS, HG_DK, HG_DV), F32)],
        scratch_shapes=[pltpu.VMEM((HG_HEADS, HG_DV, HG_DK), F32)],
        compiler_params=_cp(("parallel", "arbitrary")),
        name="hgrn_prompt")(proj, proj, proj, proj, lb, gn)


def _hgrn_step_kernel(q_ref, f_ref, i_ref, g_ref, lb_ref, gn_ref, s_ref, o_ref, so_ref):
    lb = lb_ref[0]
    H = HG_HEADS
    pad = jnp.zeros((LANE - 2 * H, HG_DK), F32)
    for j in range(q_ref.shape[0]):
        z = f_ref[j]
        q = q_ref[j]
        v = i_ref[j]
        fd = jnp.maximum(lb + (1.0 - lb) * _sigmoid(z), F_FLOOR)
        k = (1.0 - lb) * _sigmoid(-z)
        cols = jnp.concatenate([k, fd, pad], axis=0).T
        qf = q * fd
        head = lax.broadcasted_iota(jnp.int32, qf.shape, 0)
        o = jnp.sum(q * k, axis=-1, keepdims=True) * v
        for h in range(H):
            kc = cols[:, h:h + 1]
            fc = cols[:, H + h:H + h + 1]
            s_old = s_ref[0, j, h]
            so_ref[j, h] = fc * s_old + kc * v[h:h + 1]
            o = o + _dot(jnp.where(head == h, qf, 0.0).astype(BF16), s_old.astype(BF16))
        o_ref[j] = _rms(o, gn_ref[0]) * _silu(g_ref[j])


def _hgrn_step(proj3, lb3, gn, state, l, NS):
    H, W = HG_HEADS, HG_DK
    sb = 4 if NS % 4 == 0 else 1

    def part(k):
        return pl.BlockSpec((sb, H, W), lambda s: (s, k, 0))

    return pl.pallas_call(
        _hgrn_step_kernel,
        grid=(NS // sb,),
        in_specs=[part(0), part(1), part(2), part(3),
                  pl.BlockSpec((1, H, W), lambda s: (l, 0, 0)),
                  pl.BlockSpec((1, 1, W), lambda s: (l, 0, 0)),
                  pl.BlockSpec((1, sb, H, HG_DK, HG_DV), lambda s: (l, s, 0, 0, 0))],
        out_specs=[pl.BlockSpec((sb, H, HG_DV), lambda s: (s, 0, 0)),
                   pl.BlockSpec((sb, H, HG_DK, HG_DV), lambda s: (s, 0, 0, 0))],
        out_shape=[jax.ShapeDtypeStruct((NS, H, HG_DV), F32),
                   jax.ShapeDtypeStruct((NS, H, HG_DK, HG_DV), F32)],
        compiler_params=_cp(("parallel",)), name="hgrn_step")(
            proj3, proj3, proj3, proj3, lb3, gn, state)


def _prep_kernel(dq_ref, dk_ref, dv_ref, cq0_ref, cq1_ref, ckv_ref, kr_ref, cos_ref, sin_ref,
                 gq_ref, gk_ref, gcq_ref, gckv_ref, gn_ref, gr_ref, gkr_ref, wuq_ref, wuk_ref,
                 qda_ref, rda_ref, qlat_ref, qrope_ref, rmla_ref):
    cos2 = cos_ref[...]
    sin2 = sin_ref[...]
    for h in range(DA_HEADS):
        x = dq_ref[:, h * LANE:(h + 1) * LANE]
        qda_ref[:, h * LANE:(h + 1) * LANE] = _rope64(_rms64(x, gq_ref[0]), cos2, sin2)
    rda_ref[:, 0:LANE] = _rope64(_rms64(dk_ref[...], gk_ref[0]), cos2, sin2)
    rda_ref[:, LANE:2 * LANE] = dv_ref[...]
    cq = jnp.concatenate([cq0_ref[...], cq1_ref[...]], axis=1)
    qh = _dot(_rms(cq, gcq_ref[0]).astype(BF16), wuq_ref[0])
    nope = [qh[:, h * LANE:(h + 1) * LANE] for h in range(MLA_HEADS)]
    ra = qh[:, 4 * LANE:5 * LANE]
    rb = qh[:, 5 * LANE:6 * LANE]
    lo = lax.broadcasted_iota(jnp.int32, ra.shape, 1) < 64

    def halves(x):
        x2 = x * x
        return (jnp.sum(jnp.where(lo, x2, 0.0), axis=-1, keepdims=True),
                jnp.sum(jnp.where(lo, 0.0, x2), axis=-1, keepdims=True))

    sr = halves(ra) + halves(rb)
    width = float(MLA_NOPE + MLA_ROPE)
    inv = [lax.rsqrt((jnp.sum(nope[h] * nope[h], axis=-1, keepdims=True) + sr[h]) / width + EPS)
           for h in range(MLA_HEADS)]
    for h in range(MLA_HEADS):
        qn = (nope[h] * inv[h] * gn_ref[0]).astype(BF16)
        qlat_ref[:, h * MLA_KV_RANK:(h + 1) * MLA_KV_RANK] = _dot_nt(
            qn, wuk_ref[0, :, h * LANE:(h + 1) * LANE])
    qrope_ref[:, 0:LANE] = _rope64(ra * jnp.where(lo, inv[0], inv[1]) * gr_ref[0], cos2, sin2)
    qrope_ref[:, LANE:2 * LANE] = _rope64(rb * jnp.where(lo, inv[2], inv[3]) * gr_ref[0], cos2, sin2)
    rmla_ref[:, 0:MLA_KV_RANK] = _rms(ckv_ref[...], gckv_ref[0])
    kr = kr_ref[...]
    ms = jnp.sum(kr * kr, axis=-1, keepdims=True) * (1.0 / MLA_ROPE)
    krr = _rope64(kr * lax.rsqrt(ms + EPS) * gkr_ref[0], cos2, sin2)
    rmla_ref[:, MLA_KV_RANK:MLA_ROW] = krr[:, 0:MLA_ROPE]


def _prep(proj, cos2, sin2, pos_tiles, gains, wuq, wuk, l, tm):
    M = proj.shape[0]
    gq, gk, gcq, gckv, gn, gr, gkr = gains

    def col(width, idx):
        return pl.BlockSpec((tm, width), lambda i: (i, idx))

    def vec(n):
        return pl.BlockSpec((1, 1, n), lambda i: (l, 0, 0))

    tab = pl.BlockSpec((tm, LANE), lambda i: (i % pos_tiles, 0))
    return pl.pallas_call(
        _prep_kernel,
        grid=(M // tm,),
        in_specs=[col(512, 8), col(128, 36), col(128, 37), col(256, 19), col(256, 20),
                  col(256, 21), col(128, 44), tab, tab,
                  vec(128), vec(128), vec(512), vec(256), vec(128), vec(128), vec(128),
                  pl.BlockSpec((1, MLA_Q_RANK, 768), lambda i: (l, 0, 0)),
                  pl.BlockSpec((1, MLA_KV_RANK, 512), lambda i: (l, 0, 0))],
        out_specs=[pl.BlockSpec((tm, 512), lambda i: (i, 0)),
                   pl.BlockSpec((tm, DA_ROW), lambda i: (i, 0)),
                   pl.BlockSpec((tm, MLA_HEADS * MLA_KV_RANK), lambda i: (i, 0)),
                   pl.BlockSpec((tm, MLA_HEADS * MLA_ROPE), lambda i: (i, 0)),
                   pl.BlockSpec((tm, MLA_ROW), lambda i: (i, 0))],
        out_shape=[jax.ShapeDtypeStruct((M, 512), F32),
                   jax.ShapeDtypeStruct((M, DA_ROW), F32),
                   jax.ShapeDtypeStruct((M, MLA_HEADS * MLA_KV_RANK), F32),
                   jax.ShapeDtypeStruct((M, MLA_HEADS * MLA_ROPE), F32),
                   jax.ShapeDtypeStruct((M, MLA_ROW), F32)],
        compiler_params=_cp(("parallel",)), name="prep")(
            proj, proj, proj, proj, proj, proj, proj, cos2, sin2,
            gq, gk, gcq, gckv, gn, gr, gkr, wuq, wuk)


def _lam_of(lam_ref, lam_init):
    lv = lam_ref[0]
    a = jnp.sum(lv[0:1] * lv[1:2], axis=-1, keepdims=True)
    b = jnp.sum(lv[2:3] * lv[3:4], axis=-1, keepdims=True)
    return jnp.exp(a) - jnp.exp(b) + lam_init


def _split_halves(q):
    lo = lax.broadcasted_iota(jnp.int32, q.shape, 1) < 64
    return jnp.concatenate([jnp.where(lo, q, 0.0), jnp.where(lo, 0.0, q)], axis=0)


def _lanes(x, width):
    return x if width == LANE else jnp.concatenate([x] * (width // LANE), axis=1)


def _online_update(s, v, m_scr, l_scr, acc_scr):
    m_old = m_scr[...]
    m_new = jnp.maximum(m_old, jnp.max(s, axis=-1, keepdims=True))
    alpha = jnp.exp2(m_old - m_new)
    p = jnp.exp2(s - _lanes(m_new, s.shape[1]))
    l_scr[...] = alpha * l_scr[...] + jnp.sum(p, axis=-1, keepdims=True)
    acc_scr[...] = _lanes(alpha, acc_scr.shape[1]) * acc_scr[...] + _dot(p.astype(BF16), v)
    m_scr[...] = m_new


def _causal_mask(s, tq, q0, k0):
    qpos = q0 + lax.broadcasted_iota(jnp.int32, s.shape, 0) % tq
    kpos = k0 + lax.broadcasted_iota(jnp.int32, s.shape, 1)
    return jnp.where(kpos <= qpos, s, NEG)


def _causal_sweep(i, tq, step):
    tk = 2 * tq
    n_full = i // 2

    def full(j, carry):
        step(pl.multiple_of(j * tk, tk), False)
        return carry

    lax.fori_loop(0, n_full, full, 0)
    step(pl.multiple_of(n_full * tk, tk), True)


def _da_attn_kernel(q_ref, kv_ref, lam_ref, g_ref, o_ref, m_scr, l_scr, acc_scr, *, tq, lam_init):
    i = pl.program_id(1)
    q = q_ref[...]
    qh = jnp.concatenate([q[:, h * LANE:(h + 1) * LANE] for h in range(DA_HEADS)], axis=0)
    qs = (_split_halves(qh) * (DA_HALF ** -0.5 * LOG2E)).astype(BF16)
    m_scr[...] = jnp.full_like(m_scr, NEG)
    l_scr[...] = jnp.zeros_like(l_scr)
    acc_scr[...] = jnp.zeros_like(acc_scr)

    def step(r0, masked):
        kv = kv_ref[0, pl.ds(r0, 2 * tq), :]
        s = _dot_nt(qs, kv[:, 0:LANE].astype(BF16))
        if masked:
            s = _causal_mask(s, tq, i * tq, r0)
        _online_update(s, kv[:, LANE:2 * LANE].astype(BF16), m_scr, l_scr, acc_scr)

    _causal_sweep(i, tq, step)
    o = acc_scr[...] / l_scr[...]
    n = DA_HEADS * tq
    od = o[0:n] - _lam_of(lam_ref, lam_init) * o[n:2 * n]
    for h in range(DA_HEADS):
        o_ref[:, h * LANE:(h + 1) * LANE] = _rms(od[h * tq:(h + 1) * tq], g_ref[0]) * (1.0 - lam_init)


def _da_attn(q, row3, lam, g, l, lam_init, B, T):
    tq = min(256, T // 2)
    nq = T // tq
    R = 2 * DA_HEADS * tq
    return pl.pallas_call(
        functools.partial(_da_attn_kernel, tq=tq, lam_init=lam_init),
        grid=(B, nq),
        in_specs=[pl.BlockSpec((tq, 512), lambda b, i: (b * nq + i, 0)),
                  pl.BlockSpec((1, T, DA_ROW), lambda b, i: (b, 0, 0)),
                  pl.BlockSpec((1, 4, DA_HALF), lambda b, i: (l, 0, 0)),
                  pl.BlockSpec((1, 1, DA_DV), lambda b, i: (l, 0, 0))],
        out_specs=pl.BlockSpec((tq, DA_HEADS * DA_DV), lambda b, i: (b * nq + i, 0)),
        out_shape=jax.ShapeDtypeStruct((B * T, DA_HEADS * DA_DV), F32),
        scratch_shapes=[pltpu.VMEM((R, LANE), F32), pltpu.VMEM((R, LANE), F32),
                        pltpu.VMEM((R, DA_DV), F32)],
        compiler_params=_cp(("parallel", "arbitrary")), name="da_attn")(q, row3, lam, g)


def _mla_attn_kernel(ql_ref, qr_ref, kv_ref, wuv_ref, o_ref, m_scr, l_scr, acc_scr, *, tq):
    i = pl.program_id(1)
    ql = ql_ref[...]
    qr = qr_ref[...]
    R = MLA_KV_RANK
    scale = (MLA_NOPE + MLA_ROPE) ** -0.5 * LOG2E
    qls = (jnp.concatenate([ql[:, h * R:(h + 1) * R] for h in range(MLA_HEADS)], axis=0)
           * scale).astype(BF16)
    qrs = (jnp.concatenate([qr[:, h * MLA_ROPE:(h + 1) * MLA_ROPE] for h in range(MLA_HEADS)], axis=0)
           * scale).astype(BF16)
    m_scr[...] = jnp.full_like(m_scr, NEG)
    l_scr[...] = jnp.zeros_like(l_scr)
    acc_scr[...] = jnp.zeros_like(acc_scr)

    def step(r0, masked):
        kv = kv_ref[0, pl.ds(r0, 2 * tq), :]
        lat = kv[:, 0:R].astype(BF16)
        s = _dot_nt(qls, lat) + _dot_nt(qrs, kv[:, R:MLA_ROW].astype(BF16))
        if masked:
            s = _causal_mask(s, tq, i * tq, r0)
        _online_update(s, lat, m_scr, l_scr, acc_scr)

    _causal_sweep(i, tq, step)
    o = (acc_scr[...] / _lanes(l_scr[...], R)).astype(BF16)
    for h in range(MLA_HEADS):
        o_ref[:, h * LANE:(h + 1) * LANE] = _dot(o[h * tq:(h + 1) * tq],
                                                 wuv_ref[0, :, h * LANE:(h + 1) * LANE])


def _mla_attn(ql, qr, row3, wuv, l, B, T):
    tq = min(256, T // 2)
    nq = T // tq
    R = MLA_HEADS * tq
    return pl.pallas_call(
        functools.partial(_mla_attn_kernel, tq=tq),
        grid=(B, nq),
        in_specs=[pl.BlockSpec((tq, MLA_HEADS * MLA_KV_RANK), lambda b, i: (b * nq + i, 0)),
                  pl.BlockSpec((tq, MLA_HEADS * MLA_ROPE), lambda b, i: (b * nq + i, 0)),
                  pl.BlockSpec((1, T, MLA_ROW), lambda b, i: (b, 0, 0)),
                  pl.BlockSpec((1, MLA_KV_RANK, MLA_HEADS * MLA_DV), lambda b, i: (l, 0, 0))],
        out_specs=pl.BlockSpec((tq, MLA_HEADS * MLA_DV), lambda b, i: (b * nq + i, 0)),
        out_shape=jax.ShapeDtypeStruct((B * T, MLA_HEADS * MLA_DV), F32),
        scratch_shapes=[pltpu.VMEM((R, LANE), F32), pltpu.VMEM((R, LANE), F32),
                        pltpu.VMEM((R, MLA_KV_RANK), F32)],
        compiler_params=_cp(("parallel", "arbitrary")), name="mla_attn")(ql, qr, row3, wuv)


def _page_copy(cache_hbm, buf, sem, l, page, slot, p):
    return pltpu.make_async_copy(cache_hbm.at[l, page], buf.at[slot, p], sem.at[slot])


def _stream_pages(pt_ref, cache_hbm, buf, sem, l, n_pages):
    s = pl.program_id(0)
    slot = s % 2

    def fetch(seq, to_slot):
        for p in range(n_pages):
            _page_copy(cache_hbm, buf, sem, l, pt_ref[seq * n_pages + p], to_slot, p).start()

    @pl.when(s == 0)
    def _():
        fetch(0, 0)

    @pl.when(s + 1 < pl.num_programs(0))
    def _():
        fetch(s + 1, 1 - slot)

    for p in range(n_pages):
        _page_copy(cache_hbm, buf, sem, l, 0, slot, p).wait()
    return slot


def _softmax_with_new(s, sn):
    m = jnp.maximum(jnp.max(s, axis=-1, keepdims=True), sn)
    p = jnp.exp(s - m)
    pn = jnp.exp(sn - m)
    return p, pn, jnp.sum(p, axis=-1, keepdims=True) + pn


def _da_dec_kernel(pt_ref, q_ref, kn_ref, lam_ref, g_ref, cache_hbm, o_ref, buf, sem, *,
                   n_pages, l, lam_init):
    slot = _stream_pages(pt_ref, cache_hbm, buf, sem, l, n_pages)
    qf = _split_halves(q_ref[0]) * (DA_HALF ** -0.5)
    qs = qf.astype(BF16)
    kn = kn_ref[0]
    kv = buf[slot].reshape(n_pages * PAGE, DA_ROW)
    s = _dot_nt(qs, kv[:, 0:LANE].astype(BF16))
    sn = jnp.sum(qf * kn[:, 0:LANE], axis=-1, keepdims=True)
    p, pn, den = _softmax_with_new(s, sn)
    pv = pn * kn[:, LANE:2 * LANE] + _dot(p.astype(BF16), kv[:, LANE:2 * LANE].astype(BF16))
    o = pv / den
    od = o[0:DA_HEADS] - _lam_of(lam_ref, lam_init) * o[DA_HEADS:2 * DA_HEADS]
    o_ref[0] = _rms(od, g_ref[0]) * (1.0 - lam_init)


def _decode_buffers(n_pages, rows, cols):
    assert 2 * n_pages * rows * cols * 4 <= DEC_BUF_BYTES, "KV pages of one sequence must fit the buffers"
    return [pltpu.VMEM((2, n_pages, rows, cols), F32), pltpu.SemaphoreType.DMA((2,))]


def _da_decode(pt, q3, kn3, cache, lam, g, l, lam_init, NS, n_pages):
    gs = pltpu.PrefetchScalarGridSpec(
        num_scalar_prefetch=1, grid=(NS,),
        in_specs=[pl.BlockSpec((1, DA_HEADS, LANE), lambda s, pt: (s, 0, 0)),
                  pl.BlockSpec((1, 1, DA_ROW), lambda s, pt: (s, 0, 0)),
                  pl.BlockSpec((1, 4, DA_HALF), lambda s, pt: (l, 0, 0)),
                  pl.BlockSpec((1, 1, DA_DV), lambda s, pt: (l, 0, 0)),
                  pl.BlockSpec(memory_space=pl.ANY)],
        out_specs=pl.BlockSpec((1, DA_HEADS, DA_DV), lambda s, pt: (s, 0, 0)),
        scratch_shapes=_decode_buffers(n_pages, PAGE, DA_ROW))
    return pl.pallas_call(
        functools.partial(_da_dec_kernel, n_pages=n_pages, l=l, lam_init=lam_init),
        grid_spec=gs, out_shape=jax.ShapeDtypeStruct((NS, DA_HEADS, DA_DV), F32),
        compiler_params=_cp(("arbitrary",)), name="da_decode")(pt, q3, kn3, lam, g, cache)


def _mla_dec_kernel(pt_ref, ql_ref, qr_ref, kn_ref, cache_hbm, o_ref, buf, sem, *, n_pages, l):
    slot = _stream_pages(pt_ref, cache_hbm, buf, sem, l, n_pages)
    R = MLA_KV_RANK
    zpad = jnp.zeros((8 - MLA_HEADS, R), F32)
    qlf = jnp.concatenate([ql_ref[0], zpad], axis=0)
    qrf = jnp.concatenate([qr_ref[0], zpad[:, 0:MLA_ROPE]], axis=0)
    qs = jnp.concatenate([qlf, qrf], axis=1).astype(BF16)
    scale = (MLA_NOPE + MLA_ROPE) ** -0.5
    kn = kn_ref[0]
    kv_t = jnp.concatenate([buf[slot, n].astype(BF16) for n in range(n_pages)], axis=1)
    s = _dot(qs, kv_t) * scale
    sn = (jnp.sum(qlf * kn[:, 0:R], axis=-1, keepdims=True)
          + jnp.sum(qrf * kn[:, R:MLA_ROW], axis=-1, keepdims=True)) * scale
    p, pn, den = _softmax_with_new(s, sn)
    pv = pn * kn[:, 0:R] + _dot_nt(p.astype(BF16), kv_t[0:R])
    o_ref[0] = (pv / den)[0:MLA_HEADS]


def _mla_decode(pt, ql3, qr3, kn3, cache_t, l, NS, n_pages):
    gs = pltpu.PrefetchScalarGridSpec(
        num_scalar_prefetch=1, grid=(NS,),
        in_specs=[pl.BlockSpec((1, MLA_HEADS, MLA_KV_RANK), lambda s, pt: (s, 0, 0)),
                  pl.BlockSpec((1, MLA_HEADS, MLA_ROPE), lambda s, pt: (s, 0, 0)),
                  pl.BlockSpec((1, 1, MLA_ROW), lambda s, pt: (s, 0, 0)),
                  pl.BlockSpec(memory_space=pl.ANY)],
        out_specs=pl.BlockSpec((1, MLA_HEADS, MLA_KV_RANK), lambda s, pt: (s, 0, 0)),
        scratch_shapes=_decode_buffers(n_pages, MLA_ROW, PAGE))
    return pl.pallas_call(
        functools.partial(_mla_dec_kernel, n_pages=n_pages, l=l),
        grid_spec=gs, out_shape=jax.ShapeDtypeStruct((NS, MLA_HEADS, MLA_KV_RANK), F32),
        compiler_params=_cp(("arbitrary",)), name="mla_decode")(pt, ql3, qr3, kn3, cache_t)


def _uv_kernel(o_ref, w_ref, y_ref):
    o = o_ref[...].astype(BF16)
    R = MLA_KV_RANK
    for h in range(MLA_HEADS):
        y_ref[:, h * LANE:(h + 1) * LANE] = _dot(o[:, h * R:(h + 1) * R],
                                                 w_ref[0, :, h * LANE:(h + 1) * LANE])


def _uv_sample(o_lat2, wuv, l):
    NS = o_lat2.shape[0]
    return pl.pallas_call(
        _uv_kernel, grid=(1,),
        in_specs=[pl.BlockSpec((NS, MLA_HEADS * MLA_KV_RANK), lambda i: (0, 0)),
                  pl.BlockSpec((1, MLA_KV_RANK, MLA_HEADS * MLA_DV), lambda i: (l, 0, 0))],
        out_specs=pl.BlockSpec((NS, MLA_HEADS * MLA_DV), lambda i: (0, 0)),
        out_shape=jax.ShapeDtypeStruct((NS, MLA_HEADS * MLA_DV), F32),
        compiler_params=_cp(("arbitrary",)), name="mla_uv")(o_lat2, wuv)


def _outproj_kernel(oa_ref, ob_ref, oc_ref, wa_ref, wb_ref, wc_ref, x_ref, gate_ref, o_ref):
    acc = (_dot(oa_ref[...].astype(BF16), wa_ref[0]) + _dot(ob_ref[...].astype(BF16), wb_ref[0])
           + _dot(oc_ref[...].astype(BF16), wc_ref[0]))
    o_ref[...] = x_ref[...] + gate_ref[0] * acc


def _outproj(oa, ob, oc, w, x, mod, mod_idx, mod_rows, l, tm):
    M = x.shape[0]
    D = D_MODEL
    tn = 512
    nj = D // tn
    return pl.pallas_call(
        _outproj_kernel,
        grid=(M // tm, nj),
        in_specs=[pl.BlockSpec((tm, 1024), lambda i, j: (i, 0)),
                  pl.BlockSpec((tm, 512), lambda i, j: (i, 0)),
                  pl.BlockSpec((tm, 512), lambda i, j: (i, 0)),
                  pl.BlockSpec((1, 1024, tn), lambda i, j: (l, 0, j)),
                  pl.BlockSpec((1, 512, tn), lambda i, j: (l, 2, j)),
                  pl.BlockSpec((1, 512, tn), lambda i, j: (l, 3, j)),
                  pl.BlockSpec((tm, tn), lambda i, j: (i, j)),
                  pl.BlockSpec((1, mod_rows, tn), lambda i, j: (mod_idx(i), 0, 2 * nj + j))],
        out_specs=pl.BlockSpec((tm, tn), lambda i, j: (i, j)),
        out_shape=jax.ShapeDtypeStruct((M, D), F32),
        compiler_params=_cp(("parallel", "parallel")), name="outproj")(oa, ob, oc, w, w, w, x, mod)


def _ffn_up(x_ref, g_ref, sh_ref, sc_ref, wa_ref, wl_ref, h_scr, acc_scr):
    @pl.when(pl.program_id(1) == 0)
    def _():
        h = _rms(x_ref[...], g_ref[0]) * (1.0 + sc_ref[0]) + sh_ref[0]
        h_scr[...] = h.astype(BF16)
        acc_scr[...] = jnp.zeros_like(acc_scr)

    h = h_scr[...]
    return _dot(h, wa_ref[0]), _dot(h, wl_ref[0])


def _ffn_down(a, am1, am2, lin, cw_ref, cb_ref, wd_ref, x_ref, gate_ref, o_ref, acc_scr):
    cw = cw_ref[0]
    conv = cb_ref[0] + am2 * cw[0:1] + am1 * cw[1:2] + a * cw[2:3]
    act = _silu(conv) * lin
    acc_scr[...] += _dot(act.astype(BF16), wd_ref[0])

    @pl.when(pl.program_id(1) == pl.num_programs(1) - 1)
    def _():
        o_ref[...] = x_ref[...] + gate_ref[0] * acc_scr[...]


def _ffn_prompt_kernel(x_ref, g_ref, sh_ref, sc_ref, gate_ref, wa_ref, wl_ref, cw_ref, cb_ref, wd_ref,
                       o_ref, tail_ref, h_scr, acc_scr, carry_scr, *, tiles_per_seq):
    i = pl.program_id(0)
    f = pl.program_id(1)
    a, lin = _ffn_up(x_ref, g_ref, sh_ref, sc_ref, wa_ref, wl_ref, h_scr, acc_scr)
    tm = a.shape[0]

    @pl.when(i % tiles_per_seq == 0)
    def _():
        carry_scr[f] = jnp.zeros(carry_scr.shape[1:], F32)

    prev = carry_scr[f]
    row = lax.broadcasted_iota(jnp.int32, a.shape, 0)
    am1 = jnp.where(row == 0, prev[7:8], pltpu.roll(a, 1, axis=0))
    am2 = jnp.where(row == 0, prev[6:7], jnp.where(row == 1, prev[7:8], pltpu.roll(a, 2, axis=0)))
    last = a[tm - 8:tm]
    carry_scr[f] = last
    tail_ref[0] = last
    _ffn_down(a, am1, am2, lin, cw_ref, cb_ref, wd_ref, x_ref, gate_ref, o_ref, acc_scr)


def _ffn_sample_kernel(x_ref, g_ref, sh_ref, sc_ref, gate_ref, wa_ref, wl_ref, cw_ref, cb_ref, wd_ref,
                       st_ref, o_ref, tail_ref, h_scr, acc_scr):
    a, lin = _ffn_up(x_ref, g_ref, sh_ref, sc_ref, wa_ref, wl_ref, h_scr, acc_scr)
    am2 = st_ref[0, 0]
    am1 = st_ref[0, 1]
    tail_ref[0] = am1
    tail_ref[1] = a
    _ffn_down(a, am1, am2, lin, cw_ref, cb_ref, wd_ref, x_ref, gate_ref, o_ref, acc_scr)


def _ffn(x, g, mod, mod_idx, mod_rows, w_up, conv_w, conv_b, w_down, l, tm, seq_len=None, state=None):
    M = x.shape[0]
    D = D_MODEL
    tf = 512
    nf = D_FF // tf
    in_specs = [pl.BlockSpec((tm, D), lambda i, f: (i, 0)),
                pl.BlockSpec((1, 1, D), lambda i, f: (l, 0, 0)),
                pl.BlockSpec((1, mod_rows, D), lambda i, f: (mod_idx(i), 0, 3)),
                pl.BlockSpec((1, mod_rows, D), lambda i, f: (mod_idx(i), 0, 4)),
                pl.BlockSpec((1, mod_rows, D), lambda i, f: (mod_idx(i), 0, 5)),
                pl.BlockSpec((1, D, tf), lambda i, f: (l, 0, f)),
                pl.BlockSpec((1, D, tf), lambda i, f: (l, 0, nf + f)),
                pl.BlockSpec((1, 3, tf), lambda i, f: (l, 0, f)),
                pl.BlockSpec((1, 1, tf), lambda i, f: (l, 0, f)),
                pl.BlockSpec((1, tf, D), lambda i, f: (l, f, 0))]
    args = [x, g, mod, mod, mod, w_up, w_up, conv_w, conv_b, w_down]
    scratch = [pltpu.VMEM((tm, D), BF16), pltpu.VMEM((tm, D), F32)]
    if state is None:
        kern = functools.partial(_ffn_prompt_kernel, tiles_per_seq=seq_len // tm)
        tail_spec = pl.BlockSpec((1, 8, tf), lambda i, f: (i, 0, f))
        tail_shape = jax.ShapeDtypeStruct((M // tm, 8, D_FF), F32)
        scratch = scratch + [pltpu.VMEM((nf, 8, tf), F32)]
    else:
        kern = _ffn_sample_kernel
        in_specs.append(pl.BlockSpec((1, 2, tm, tf), lambda i, f: (l, 0, 0, f)))
        args.append(state)
        tail_spec = pl.BlockSpec((2, tm, tf), lambda i, f: (0, 0, f))
        tail_shape = jax.ShapeDtypeStruct((2, M, D_FF), F32)
    return pl.pallas_call(
        kern, grid=(M // tm, nf), in_specs=in_specs,
        out_specs=[pl.BlockSpec((tm, D), lambda i, f: (i, 0)), tail_spec],
        out_shape=[jax.ShapeDtypeStruct((M, D), F32), tail_shape],
        scratch_shapes=scratch,
        compiler_params=_cp(("arbitrary", "arbitrary")), name="convffn")(*args)


def _rope_tables(pos):
    inv = ROPE_THETA ** (-jnp.arange(0, 64, 2, dtype=F32) / 64)
    ang = pos.astype(F32)[:, None] * inv[None, :]
    cos, sin = jnp.cos(ang), jnp.sin(ang)
    cos2 = jnp.tile(cos, (1, 4))
    sin2 = jnp.tile(jnp.concatenate([-sin, sin], axis=1), (1, 2))
    return cos2, sin2


def kernel(x_prompt, x_sample, cache_diff_kv, cache_mla_latent, state_hgrn, state_ffn_conv, page_table, c_prompt, c_sample, norm_mix_g, norm_ffn_g, w_ada, b_ada, w_in, hg_lb_logits, hg_norm_g, da_lam, da_qnorm_g, da_knorm_g, da_onorm_g, mla_cq_norm_g, mla_ckv_norm_g, mla_w_uq, mla_w_uk, mla_w_uv, mla_qnorm_g, mla_knorm_g, w_out, w_up, conv_w, conv_b, w_down):
    B, T, D = x_prompt.shape
    NS = x_sample.shape[0]
    L = w_in.shape[0]
    n_pages = page_table.shape[1]
    past_len = n_pages * PAGE
    assert x_sample.shape[1] == 1 and D == D_MODEL

    w_in_b = jnp.pad(w_in.astype(BF16), ((0, 0), (0, 0), (0, PROJ_PAD - PROJ_WIDTH)))
    w_out_b = w_out.astype(BF16)
    w_up_b = w_up.astype(BF16)
    w_down_b = w_down.astype(BF16)
    wq4 = mla_w_uq.reshape(L, MLA_Q_RANK, MLA_HEADS, MLA_NOPE + MLA_ROPE)
    wuq_b = jnp.concatenate([wq4[..., :MLA_NOPE].reshape(L, MLA_Q_RANK, -1),
                             wq4[..., MLA_NOPE:].reshape(L, MLA_Q_RANK, -1)], axis=-1).astype(BF16)
    wuk_b = mla_w_uk.astype(BF16)
    wuv_b = mla_w_uv.astype(BF16)

    def vec(a):
        return a.reshape(L, 1, -1)

    gains = (vec(jnp.tile(da_qnorm_g, (1, 2))), vec(jnp.tile(da_knorm_g, (1, 2))),
             vec(mla_cq_norm_g), vec(mla_ckv_norm_g), vec(mla_qnorm_g[:, :MLA_NOPE]),
             vec(jnp.tile(mla_qnorm_g[:, MLA_NOPE:], (1, 2))),
             vec(jnp.pad(mla_knorm_g, ((0, 0), (0, LANE - MLA_ROPE)))))
    g_mix, g_ffn = vec(norm_mix_g), vec(norm_ffn_g)
    g_hg, g_da = vec(hg_norm_g), vec(da_onorm_g)
    conv_b3 = vec(conv_b)
    conv_state = state_ffn_conv.transpose(0, 2, 1, 3)
    cache_mla_t = jnp.swapaxes(cache_mla_latent, 2, 3)

    lb = _lower_bounds(hg_lb_logits)
    lb_p = lb.reshape(L, 1, HG_HEADS * HG_DK)
    lb_s = lb.reshape(L, HG_HEADS, HG_DK)

    n_rows = -(-(NS + B) // 8) * 8
    c_all = jnp.concatenate([c_sample, c_prompt, jnp.zeros((n_rows - NS - B, D), F32)], axis=0)
    mod_s = _ada(c_all, w_ada, vec(b_ada))
    mod_p = mod_s[:, NS:NS + B].reshape(L * B, 1, 6 * D)

    cos_p, sin_p = _rope_tables(jnp.arange(T))
    cos_s, sin_s = _rope_tables(jnp.full((NS,), past_len))
    pt_flat = page_table.reshape(-1)

    tm_p = min(1024, T)
    tm_ffn = min(512, T)
    tm_prep = min(256, T)
    xp = x_prompt.reshape(B * T, D)
    xs = x_sample.reshape(NS, D)
    outs = [[] for _ in range(8)]
    for l in range(L):
        lam_init = 0.8 - 0.6 * math.exp(-0.3 * l)

        def idx_p(tm):
            return lambda i: l * B + (i * tm) // T

        def idx_s(i):
            return l

        proj = _inproj(xp, g_mix, mod_p, idx_p(tm_p), 1, w_in_b, l, tm_p)
        o_a, st_p = _hgrn_prompt(proj, lb_p, g_hg, l, B, T)
        q_da, row_da, q_lat, q_rope, row_mla = _prep(
            proj, cos_p, sin_p, T // tm_prep, gains, wuq_b, wuk_b, l, tm_prep)
        o_b = _da_attn(q_da, row_da.reshape(B, T, DA_ROW), da_lam, g_da, l, lam_init, B, T)
        o_c = _mla_attn(q_lat, q_rope, row_mla.reshape(B, T, MLA_ROW), wuv_b, l, B, T)
        xp = _outproj(o_a, o_b, o_c, w_out_b, xp, mod_p, idx_p(tm_p), 1, l, tm_p)
        xp, tail_p = _ffn(xp, g_ffn, mod_p, idx_p(tm_ffn), 1, w_up_b, conv_w, conv_b3, w_down_b, l,
                          tm_ffn, seq_len=T)
        tps = T // tm_ffn
        cv_p = tail_p.reshape(B, tps, 8, D_FF)[:, tps - 1, 6:8]

        proj_s = _inproj(xs, g_mix, mod_s, idx_s, NS, w_in_b, l, NS)
        oa_s, st_s = _hgrn_step(proj_s.reshape(NS, PROJ_PAD // LANE, LANE), lb_s, g_hg, state_hgrn, l, NS)
        qda_s, rda_s, qlat_s, qrope_s, rmla_s = _prep(
            proj_s, cos_s, sin_s, 1, gains, wuq_b, wuk_b, l, NS)
        ob_s = _da_decode(pt_flat, qda_s.reshape(NS, DA_HEADS, LANE), rda_s.reshape(NS, 1, DA_ROW),
                          cache_diff_kv, da_lam, g_da, l, lam_init, NS, n_pages)
        olat_s = _mla_decode(pt_flat, qlat_s.reshape(NS, MLA_HEADS, MLA_KV_RANK),
                             qrope_s.reshape(NS, MLA_HEADS, MLA_ROPE), rmla_s.reshape(NS, 1, MLA_ROW),
                             cache_mla_t, l, NS, n_pages)
        oc_s = _uv_sample(olat_s.reshape(NS, MLA_HEADS * MLA_KV_RANK), wuv_b, l)
        xs = _outproj(oa_s.reshape(NS, HG_HEADS * HG_DV), ob_s.reshape(NS, DA_HEADS * DA_DV), oc_s,
                      w_out_b, xs, mod_s, idx_s, NS, l, NS)
        xs, tail_s = _ffn(xs, g_ffn, mod_s, idx_s, NS, w_up_b, conv_w, conv_b3, w_down_b, l, NS,
                          state=conv_state)

        for lst, val in zip(outs, (row_da.reshape(B, T, DA_ROW), rda_s.reshape(NS, 1, DA_ROW),
                                   row_mla.reshape(B, T, MLA_ROW), rmla_s.reshape(NS, 1, MLA_ROW),
                                   st_p, st_s, cv_p, tail_s.transpose(1, 0, 2))):
            lst.append(val)

    return (xp.reshape(B, T, D), xs.reshape(NS, 1, D)) + tuple(jnp.stack(o) for o in outs)
```

```python
import functools
import math

import jax
import jax.numpy as jnp
from jax import lax
from jax.experimental import pallas as pl
from jax.experimental.pallas import tpu as pltpu

F32 = jnp.float32
BF16 = jnp.bfloat16

D_MODEL = 2048
HG_HEADS = 8
HG_DK = 128
HG_DV = 128
DA_HEADS = 4
DA_HALF = 64
DA_DV = 128
DA_ROW = 256
MLA_HEADS = 4
MLA_Q_RANK = 512
MLA_KV_RANK = 256
MLA_NOPE = 128
MLA_ROPE = 64
MLA_DV = 128
MLA_ROW = MLA_KV_RANK + MLA_ROPE
D_FF = 5632
PAGE = 128
ROPE_THETA = 10000.0
EPS = 1e-6
NEG = -1e30
F_FLOOR = 1e-30
LOG2E = 1.4426950408889634
PROJ_WIDTH = 5696
PROJ_PAD = 6144
LANE = 128
HG_SUB = 16
DEC_BUF_BYTES = 24 * 1024 * 1024
VMEM_LIMIT = 48 * 1024 * 1024


def _cp(sem, vmem=VMEM_LIMIT):
    return pltpu.CompilerParams(dimension_semantics=sem, vmem_limit_bytes=vmem)


def _sigmoid(x):
    return 1.0 / (1.0 + jnp.exp(-x))


def _silu(x):
    return x * _sigmoid(x)


def _rms(x, g):
    ms = jnp.mean(x * x, axis=-1, keepdims=True)
    return x * lax.rsqrt(ms + EPS) * g


def _rms64(x, g2):
    lo = lax.broadcasted_iota(jnp.int32, x.shape, 1) < 64
    x2 = x * x
    s_lo = jnp.sum(jnp.where(lo, x2, 0.0), axis=-1, keepdims=True)
    s_hi = jnp.sum(jnp.where(lo, 0.0, x2), axis=-1, keepdims=True)
    ms = jnp.where(lo, s_lo, s_hi) * (1.0 / 64.0)
    return x * lax.rsqrt(ms + EPS) * g2


def _rope64(x, cos2, sin2):
    lane = lax.broadcasted_iota(jnp.int32, x.shape, 1)
    first = (lane % 64) < 32
    sw = jnp.where(first, pltpu.roll(x, LANE - 32, axis=1), pltpu.roll(x, 32, axis=1))
    return x * cos2 + sw * sin2


def _dot(a, b):
    return jnp.dot(a, b, preferred_element_type=F32)


def _dot_nt(a, b):
    return lax.dot_general(a, b, (((1,), (1,)), ((), ())), preferred_element_type=F32)


def _dot_tn(a, b):
    return lax.dot_general(a, b, (((0,), (0,)), ((), ())), preferred_element_type=F32)


def _lb_kernel(x_ref, o_ref):
    x = x_ref[...]
    n = x.shape[0]
    m = jnp.max(x, axis=0, keepdims=True)
    e = jnp.exp(x - m)
    p = e / jnp.sum(e, axis=0, keepdims=True)
    rows = []
    c = None
    for l in range(n):
        c = p[l:l + 1] if c is None else c + p[l:l + 1]
        rows.append(c - p[0:1])
    o_ref[...] = jnp.concatenate(rows, axis=0)


def _lower_bounds(logits):
    return pl.pallas_call(
        _lb_kernel, out_shape=jax.ShapeDtypeStruct(logits.shape, F32), name="lower_bounds")(logits)


def _ada_kernel(c_ref, w_ref, b_ref, o_ref):
    a = _silu(c_ref[...]).astype(BF16)
    o_ref[0] = _dot(a, w_ref[0].astype(BF16)) + b_ref[0]


def _ada(c_all, w_ada, b_ada):
    L, _, N = w_ada.shape
    R = c_all.shape[0]
    tn = 1024
    return pl.pallas_call(
        _ada_kernel,
        grid=(L, N // tn),
        in_specs=[pl.BlockSpec((R, D_MODEL), lambda l, j: (0, 0)),
                  pl.BlockSpec((1, D_MODEL, tn), lambda l, j: (l, 0, j)),
                  pl.BlockSpec((1, 1, tn), lambda l, j: (l, 0, j))],
        out_specs=pl.BlockSpec((1, R, tn), lambda l, j: (l, 0, j)),
        out_shape=jax.ShapeDtypeStruct((L, R, N), F32),
        compiler_params=_cp(("parallel", "parallel")), name="adaln")(c_all, w_ada, b_ada)


def _inproj_kernel(x_ref, g_ref, sh_ref, sc_ref, w_ref, o_ref, h_scr):
    @pl.when(pl.program_id(1) == 0)
    def _():
        h = _rms(x_ref[...], g_ref[0]) * (1.0 + sc_ref[0]) + sh_ref[0]
        h_scr[...] = h.astype(BF16)

    o_ref[...] = _dot(h_scr[...], w_ref[0])


def _inproj(x, g, mod, mod_idx, mod_rows, w, l, tm):
    M = x.shape[0]
    NP = w.shape[2]
    tn = 512
    D = D_MODEL
    return pl.pallas_call(
        _inproj_kernel,
        grid=(M // tm, NP // tn),
        in_specs=[pl.BlockSpec((tm, D), lambda i, j: (i, 0)),
                  pl.BlockSpec((1, 1, D), lambda i, j: (l, 0, 0)),
                  pl.BlockSpec((1, mod_rows, D), lambda i, j: (mod_idx(i), 0, 0)),
                  pl.BlockSpec((1, mod_rows, D), lambda i, j: (mod_idx(i), 0, 1)),
                  pl.BlockSpec((1, D, tn), lambda i, j: (l, 0, j))],
        out_specs=pl.BlockSpec((tm, tn), lambda i, j: (i, j)),
        out_shape=jax.ShapeDtypeStruct((M, NP), F32),
        scratch_shapes=[pltpu.VMEM((tm, D), BF16)],
        compiler_params=_cp(("parallel", "arbitrary")), name="inproj")(x, g, mod, mod, w)


def _hgrn_kernel(q_ref, f_ref, i_ref, g_ref, lb_ref, gn_ref, o_ref, st_ref, s_scr, *, n_sub):
    c = HG_SUB
    t = pl.program_id(1)

    @pl.when(t == 0)
    def _():
        s_scr[...] = jnp.zeros_like(s_scr)

    gn = gn_ref[0]
    row = lax.broadcasted_iota(jnp.int32, (c, HG_DK), 0)
    ones = jnp.ones((HG_DK, LANE), BF16)

    def body(ci, carry):
        r0 = pl.multiple_of(ci * c, c)
        rows = pl.ds(r0, c)
        for h in range(HG_HEADS):
            cols = slice(h * HG_DK, (h + 1) * HG_DK)
            lb = lb_ref[0, :, cols]
            z = f_ref[rows, cols]
            q = q_ref[rows, cols]
            v = i_ref[rows, cols]
            f = lb + (1.0 - lb) * _sigmoid(z)
            k = (1.0 - lb) * _sigmoid(-z)
            b = jnp.log(jnp.maximum(f, F_FLOOR))
            s = 1
            while s < c:
                b = b + jnp.where(row >= s, pltpu.roll(b, s, axis=0), 0.0)
                s *= 2
            b = b * LOG2E
            bl = b[c - 1:c]
            st = s_scr[h]
            o = _dot_nt((q * jnp.exp2(b)).astype(BF16), st.astype(BF16))
            w = jnp.concatenate(
                [(jnp.where(row >= s, jnp.exp2(b - b[s:s + 1]), 0.0) * (q * k[s:s + 1])).astype(BF16)
                 for s in range(c)], axis=0)
            a = _dot(w, ones)
            for s in range(c):
                o = o + a[s * c:(s + 1) * c] * v[s:s + 1]
            ke = k * jnp.exp2(bl - b)
            s_scr[h] = st * jnp.exp2(bl) + _dot_tn(v.astype(BF16), ke.astype(BF16))
            o_ref[rows, cols] = _rms(o, gn) * _silu(g_ref[rows, cols])
        return carry

    lax.fori_loop(0, n_sub, body, 0)

    @pl.when(t == pl.num_programs(1) - 1)
    def _():
        for h in range(HG_HEADS):
            st_ref[0, h] = s_scr[h].T


def _hgrn_prompt(proj, lb, gn, l, B, T):
    tc = min(256, T)
    nt = T // tc
    W = HG_HEADS * HG_DK

    def col(k):
        return pl.BlockSpec((tc, W), lambda b, t: (b * nt + t, k))

    return pl.pallas_call(
        functools.partial(_hgrn_kernel, n_sub=tc // HG_SUB),
        grid=(B, nt),
        in_specs=[col(0), col(1), col(2), col(3),
                  pl.BlockSpec((1, 1, W), lambda b, t: (l, 0, 0)),
                  pl.BlockSpec((1, 1, HG_DV), lambda b, t: (l, 0, 0))],
        out_specs=[pl.BlockSpec((tc, W), lambda b, t: (b * nt + t, 0)),
                   pl.BlockSpec((1, HG_HEADS, HG_DK, HG_DV), lambda b, t: (b, 0, 0, 0))],
        out_shape=[jax.ShapeDtypeStruct((B * T, HG_HEADS * HG_DV), F32),
                   jax.ShapeDtypeStruct((B, HG_HEADS, HG_DK, HG_DV), F32)],
        scratch_shapes=[pltpu.VMEM((HG_HEADS, HG_DV, HG_DK), F32)],
        compiler_params=_cp(("parallel", "arbitrary")),
        name="hgrn_prompt")(proj, proj, proj, proj, lb, gn)


def _hgrn_step_kernel(q_ref, f_ref, i_ref, g_ref, lb_ref, gn_ref, s_ref, all_ref, o_ref, so_ref):
    del all_ref
    lb = lb_ref[0]
    H = HG_HEADS
    pad = jnp.zeros((LANE - 2 * H, HG_DK), F32)
    for j in range(q_ref.shape[0]):
        z = f_ref[j]
        q = q_ref[j]
        v = i_ref[j]
        fd = jnp.maximum(lb + (1.0 - lb) * _sigmoid(z), F_FLOOR)
        k = (1.0 - lb) * _sigmoid(-z)
        cols = jnp.concatenate([k, fd, pad], axis=0).T
        qf = q * fd
        head = lax.broadcasted_iota(jnp.int32, qf.shape, 0)
        o = jnp.sum(q * k, axis=-1, keepdims=True) * v
        for h in range(H):
            kc = cols[:, h:h + 1]
            fc = cols[:, H + h:H + h + 1]
            s_old = s_ref[0, j, h]
            so_ref[0, j, h] = fc * s_old + kc * v[h:h + 1]
            o = o + _dot(jnp.where(head == h, qf, 0.0).astype(BF16), s_old.astype(BF16))
        o_ref[j] = _rms(o, gn_ref[0]) * _silu(g_ref[j])


def _hgrn_step(proj3, lb3, gn, state, new_state, l, NS):
    H, W = HG_HEADS, HG_DK
    sb = 4 if NS % 4 == 0 else 1

    def part(k):
        return pl.BlockSpec((sb, H, W), lambda s: (s, k, 0))

    in_specs = [part(0), part(1), part(2), part(3),
                pl.BlockSpec((1, H, W), lambda s: (l, 0, 0)),
                pl.BlockSpec((1, 1, W), lambda s: (l, 0, 0)),
                pl.BlockSpec((1, sb, H, HG_DK, HG_DV), lambda s: (l, s, 0, 0, 0)),
                pl.BlockSpec(memory_space=pl.ANY)]
    args = [proj3, proj3, proj3, proj3, lb3, gn, state, new_state]
    aliases = {len(args) - 1: 1}
    return pl.pallas_call(
        _hgrn_step_kernel,
        grid=(NS // sb,),
        in_specs=in_specs,
        out_specs=[pl.BlockSpec((sb, H, HG_DV), lambda s: (s, 0, 0)),
                   pl.BlockSpec((1, sb, H, HG_DK, HG_DV), lambda s: (l, s, 0, 0, 0))],
        out_shape=[jax.ShapeDtypeStruct((NS, H, HG_DV), F32),
                   jax.ShapeDtypeStruct(state.shape, F32)],
        input_output_aliases=aliases,
        compiler_params=_cp(("parallel",)), name="hgrn_step")(*args)


def _prep_kernel(dq_ref, dk_ref, dv_ref, cq0_ref, cq1_ref, ckv_ref, kr_ref, cos_ref, sin_ref,
                 gq_ref, gk_ref, gcq_ref, gckv_ref, gn_ref, gr_ref, gkr_ref, wuq_ref, wuk_ref,
                 qda_ref, rda_ref, qlat_ref, qrope_ref, rmla_ref):
    cos2 = cos_ref[...]
    sin2 = sin_ref[...]
    for h in range(DA_HEADS):
        x = dq_ref[:, h * LANE:(h + 1) * LANE]
        qda_ref[:, h * LANE:(h + 1) * LANE] = _rope64(_rms64(x, gq_ref[0]), cos2, sin2)
    rda_ref[:, 0:LANE] = _rope64(_rms64(dk_ref[...], gk_ref[0]), cos2, sin2)
    rda_ref[:, LANE:2 * LANE] = dv_ref[...]
    cq = jnp.concatenate([cq0_ref[...], cq1_ref[...]], axis=1)
    qh = _dot(_rms(cq, gcq_ref[0]).astype(BF16), wuq_ref[0])
    nope = [qh[:, h * LANE:(h + 1) * LANE] for h in range(MLA_HEADS)]
    ra = qh[:, 4 * LANE:5 * LANE]
    rb = qh[:, 5 * LANE:6 * LANE]
    lo = lax.broadcasted_iota(jnp.int32, ra.shape, 1) < 64

    def halves(x):
        x2 = x * x
        return (jnp.sum(jnp.where(lo, x2, 0.0), axis=-1, keepdims=True),
                jnp.sum(jnp.where(lo, 0.0, x2), axis=-1, keepdims=True))

    sr = halves(ra) + halves(rb)
    width = float(MLA_NOPE + MLA_ROPE)
    inv = [lax.rsqrt((jnp.sum(nope[h] * nope[h], axis=-1, keepdims=True) + sr[h]) / width + EPS)
           for h in range(MLA_HEADS)]
    for h in range(MLA_HEADS):
        qn = (nope[h] * inv[h] * gn_ref[0]).astype(BF16)
        qlat_ref[:, h * MLA_KV_RANK:(h + 1) * MLA_KV_RANK] = _dot_nt(
            qn, wuk_ref[0, :, h * LANE:(h + 1) * LANE])
    qrope_ref[:, 0:LANE] = _rope64(ra * jnp.where(lo, inv[0], inv[1]) * gr_ref[0], cos2, sin2)
    qrope_ref[:, LANE:2 * LANE] = _rope64(rb * jnp.where(lo, inv[2], inv[3]) * gr_ref[0], cos2, sin2)
    rmla_ref[:, 0:MLA_KV_RANK] = _rms(ckv_ref[...], gckv_ref[0])
    kr = kr_ref[...]
    ms = jnp.sum(kr * kr, axis=-1, keepdims=True) * (1.0 / MLA_ROPE)
    krr = _rope64(kr * lax.rsqrt(ms + EPS) * gkr_ref[0], cos2, sin2)
    rmla_ref[:, MLA_KV_RANK:MLA_ROW] = krr[:, 0:MLA_ROPE]


def _prep(proj, cos2, sin2, pos_tiles, gains, wuq, wuk, l, tm):
    M = proj.shape[0]
    gq, gk, gcq, gckv, gn, gr, gkr = gains

    def col(width, idx):
        return pl.BlockSpec((tm, width), lambda i: (i, idx))

    def vec(n):
        return pl.BlockSpec((1, 1, n), lambda i: (l, 0, 0))

    tab = pl.BlockSpec((tm, LANE), lambda i: (i % pos_tiles, 0))
    return pl.pallas_call(
        _prep_kernel,
        grid=(M // tm,),
        in_specs=[col(512, 8), col(128, 36), col(128, 37), col(256, 19), col(256, 20),
                  col(256, 21), col(128, 44), tab, tab,
                  vec(128), vec(128), vec(512), vec(256), vec(128), vec(128), vec(128),
                  pl.BlockSpec((1, MLA_Q_RANK, 768), lambda i: (l, 0, 0)),
                  pl.BlockSpec((1, MLA_KV_RANK, 512), lambda i: (l, 0, 0))],
        out_specs=[pl.BlockSpec((tm, 512), lambda i: (i, 0)),
                   pl.BlockSpec((tm, DA_ROW), lambda i: (i, 0)),
                   pl.BlockSpec((tm, MLA_HEADS * MLA_KV_RANK), lambda i: (i, 0)),
                   pl.BlockSpec((tm, MLA_HEADS * MLA_ROPE), lambda i: (i, 0)),
                   pl.BlockSpec((tm, MLA_ROW), lambda i: (i, 0))],
        out_shape=[jax.ShapeDtypeStruct((M, 512), F32),
                   jax.ShapeDtypeStruct((M, DA_ROW), F32),
                   jax.ShapeDtypeStruct((M, MLA_HEADS * MLA_KV_RANK), F32),
                   jax.ShapeDtypeStruct((M, MLA_HEADS * MLA_ROPE), F32),
                   jax.ShapeDtypeStruct((M, MLA_ROW), F32)],
        compiler_params=_cp(("parallel",)), name="prep")(
            proj, proj, proj, proj, proj, proj, proj, cos2, sin2,
            gq, gk, gcq, gckv, gn, gr, gkr, wuq, wuk)


def _lam_of(lam_ref, lam_init):
    lv = lam_ref[0]
    a = jnp.sum(lv[0:1] * lv[1:2], axis=-1, keepdims=True)
    b = jnp.sum(lv[2:3] * lv[3:4], axis=-1, keepdims=True)
    return jnp.exp(a) - jnp.exp(b) + lam_init


def _split_halves(q):
    lo = lax.broadcasted_iota(jnp.int32, q.shape, 1) < 64
    return jnp.concatenate([jnp.where(lo, q, 0.0), jnp.where(lo, 0.0, q)], axis=0)


def _lanes(x, width):
    return x if width == LANE else jnp.concatenate([x] * (width // LANE), axis=1)


def _online_update(s, v, m_scr, l_scr, acc_scr):
    m_old = m_scr[...]
    m_new = jnp.maximum(m_old, jnp.max(s, axis=-1, keepdims=True))
    alpha = jnp.exp2(m_old - m_new)
    p = jnp.exp2(s - _lanes(m_new, s.shape[1]))
    l_scr[...] = alpha * l_scr[...] + jnp.sum(p, axis=-1, keepdims=True)
    acc_scr[...] = _lanes(alpha, acc_scr.shape[1]) * acc_scr[...] + _dot(p.astype(BF16), v)
    m_scr[...] = m_new


def _causal_mask(s, tq, q0, k0):
    qpos = q0 + lax.broadcasted_iota(jnp.int32, s.shape, 0) % tq
    kpos = k0 + lax.broadcasted_iota(jnp.int32, s.shape, 1)
    return jnp.where(kpos <= qpos, s, NEG)


def _causal_sweep(i, tq, step):
    tk = 2 * tq
    n_full = i // 2

    def full(j, carry):
        step(pl.multiple_of(j * tk, tk), False)
        return carry

    lax.fori_loop(0, n_full, full, 0)
    step(pl.multiple_of(n_full * tk, tk), True)


def _da_attn_kernel(q_ref, kv_ref, lam_ref, g_ref, o_ref, m_scr, l_scr, acc_scr, *, tq, lam_init):
    i = pl.program_id(1)
    q = q_ref[...]
    qh = jnp.concatenate([q[:, h * LANE:(h + 1) * LANE] for h in range(DA_HEADS)], axis=0)
    qs = (_split_halves(qh) * (DA_HALF ** -0.5 * LOG2E)).astype(BF16)
    m_scr[...] = jnp.full_like(m_scr, NEG)
    l_scr[...] = jnp.zeros_like(l_scr)
    acc_scr[...] = jnp.zeros_like(acc_scr)

    def step(r0, masked):
        kv = kv_ref[0, pl.ds(r0, 2 * tq), :]
        s = _dot_nt(qs, kv[:, 0:LANE].astype(BF16))
        if masked:
            s = _causal_mask(s, tq, i * tq, r0)
        _online_update(s, kv[:, LANE:2 * LANE].astype(BF16), m_scr, l_scr, acc_scr)

    _causal_sweep(i, tq, step)
    o = acc_scr[...] / l_scr[...]
    n = DA_HEADS * tq
    od = o[0:n] - _lam_of(lam_ref, lam_init) * o[n:2 * n]
    for h in range(DA_HEADS):
        o_ref[:, h * LANE:(h + 1) * LANE] = _rms(od[h * tq:(h + 1) * tq], g_ref[0]) * (1.0 - lam_init)


def _da_attn(q, row3, lam, g, l, lam_init, B, T):
    tq = min(256, T // 2)
    nq = T // tq
    R = 2 * DA_HEADS * tq
    return pl.pallas_call(
        functools.partial(_da_attn_kernel, tq=tq, lam_init=lam_init),
        grid=(B, nq),
        in_specs=[pl.BlockSpec((tq, 512), lambda b, i: (b * nq + i, 0)),
                  pl.BlockSpec((1, T, DA_ROW), lambda b, i: (b, 0, 0)),
                  pl.BlockSpec((1, 4, DA_HALF), lambda b, i: (l, 0, 0)),
                  pl.BlockSpec((1, 1, DA_DV), lambda b, i: (l, 0, 0))],
        out_specs=pl.BlockSpec((tq, DA_HEADS * DA_DV), lambda b, i: (b * nq + i, 0)),
        out_shape=jax.ShapeDtypeStruct((B * T, DA_HEADS * DA_DV), F32),
        scratch_shapes=[pltpu.VMEM((R, LANE), F32), pltpu.VMEM((R, LANE), F32),
                        pltpu.VMEM((R, DA_DV), F32)],
        compiler_params=_cp(("parallel", "arbitrary")), name="da_attn")(q, row3, lam, g)


def _mla_attn_kernel(ql_ref, qr_ref, kv_ref, wuv_ref, o_ref, m_scr, l_scr, acc_scr, *, tq):
    i = pl.program_id(1)
    ql = ql_ref[...]
    qr = qr_ref[...]
    R = MLA_KV_RANK
    scale = (MLA_NOPE + MLA_ROPE) ** -0.5 * LOG2E
    qls = (jnp.concatenate([ql[:, h * R:(h + 1) * R] for h in range(MLA_HEADS)], axis=0)
           * scale).astype(BF16)
    qrs = (jnp.concatenate([qr[:, h * MLA_ROPE:(h + 1) * MLA_ROPE] for h in range(MLA_HEADS)], axis=0)
           * scale).astype(BF16)
    m_scr[...] = jnp.full_like(m_scr, NEG)
    l_scr[...] = jnp.zeros_like(l_scr)
    acc_scr[...] = jnp.zeros_like(acc_scr)

    def step(r0, masked):
        kv = kv_ref[0, pl.ds(r0, 2 * tq), :]
        lat = kv[:, 0:R].astype(BF16)
        s = _dot_nt(qls, lat) + _dot_nt(qrs, kv[:, R:MLA_ROW].astype(BF16))
        if masked:
            s = _causal_mask(s, tq, i * tq, r0)
        _online_update(s, lat, m_scr, l_scr, acc_scr)

    _causal_sweep(i, tq, step)
    o = (acc_scr[...] / _lanes(l_scr[...], R)).astype(BF16)
    for h in range(MLA_HEADS):
        o_ref[:, h * LANE:(h + 1) * LANE] = _dot(o[h * tq:(h + 1) * tq],
                                                 wuv_ref[0, :, h * LANE:(h + 1) * LANE])


def _mla_attn(ql, qr, row3, wuv, l, B, T):
    tq = min(256, T // 2)
    nq = T // tq
    R = MLA_HEADS * tq
    return pl.pallas_call(
        functools.partial(_mla_attn_kernel, tq=tq),
        grid=(B, nq),
        in_specs=[pl.BlockSpec((tq, MLA_HEADS * MLA_KV_RANK), lambda b, i: (b * nq + i, 0)),
                  pl.BlockSpec((tq, MLA_HEADS * MLA_ROPE), lambda b, i: (b * nq + i, 0)),
                  pl.BlockSpec((1, T, MLA_ROW), lambda b, i: (b, 0, 0)),
                  pl.BlockSpec((1, MLA_KV_RANK, MLA_HEADS * MLA_DV), lambda b, i: (l, 0, 0))],
        out_specs=pl.BlockSpec((tq, MLA_HEADS * MLA_DV), lambda b, i: (b * nq + i, 0)),
        out_shape=jax.ShapeDtypeStruct((B * T, MLA_HEADS * MLA_DV), F32),
        scratch_shapes=[pltpu.VMEM((R, LANE), F32), pltpu.VMEM((R, LANE), F32),
                        pltpu.VMEM((R, MLA_KV_RANK), F32)],
        compiler_params=_cp(("parallel", "arbitrary")), name="mla_attn")(ql, qr, row3, wuv)


def _page_copy(cache_hbm, buf, sem, l, page, slot, p):
    return pltpu.make_async_copy(cache_hbm.at[l, page], buf.at[slot, p], sem.at[slot])


def _stream_pages(pt_ref, cache_hbm, buf, sem, l, n_pages):
    s = pl.program_id(0)
    slot = s % 2

    def fetch(seq, to_slot):
        for p in range(n_pages):
            _page_copy(cache_hbm, buf, sem, l, pt_ref[seq * n_pages + p], to_slot, p).start()

    @pl.when(s == 0)
    def _():
        fetch(0, 0)

    @pl.when(s + 1 < pl.num_programs(0))
    def _():
        fetch(s + 1, 1 - slot)

    for p in range(n_pages):
        _page_copy(cache_hbm, buf, sem, l, 0, slot, p).wait()
    return slot


def _softmax_with_new(s, sn):
    m = jnp.maximum(jnp.max(s, axis=-1, keepdims=True), sn)
    p = jnp.exp(s - m)
    pn = jnp.exp(sn - m)
    return p, pn, jnp.sum(p, axis=-1, keepdims=True) + pn


def _da_dec_kernel(pt_ref, q_ref, kn_ref, lam_ref, g_ref, cache_hbm, o_ref, buf, sem, *,
                   n_pages, l, lam_init):
    slot = _stream_pages(pt_ref, cache_hbm, buf, sem, l, n_pages)
    qf = _split_halves(q_ref[0]) * (DA_HALF ** -0.5)
    qs = qf.astype(BF16)
    kn = kn_ref[0]
    kv = buf[slot].reshape(n_pages * PAGE, DA_ROW)
    s = _dot_nt(qs, kv[:, 0:LANE].astype(BF16))
    sn = jnp.sum(qf * kn[:, 0:LANE], axis=-1, keepdims=True)
    p, pn, den = _softmax_with_new(s, sn)
    pv = pn * kn[:, LANE:2 * LANE] + _dot(p.astype(BF16), kv[:, LANE:2 * LANE].astype(BF16))
    o = pv / den
    od = o[0:DA_HEADS] - _lam_of(lam_ref, lam_init) * o[DA_HEADS:2 * DA_HEADS]
    o_ref[0] = _rms(od, g_ref[0]) * (1.0 - lam_init)


def _decode_buffers(n_pages, rows, cols):
    assert 2 * n_pages * rows * cols * 4 <= DEC_BUF_BYTES, "KV pages of one sequence must fit the buffers"
    return [pltpu.VMEM((2, n_pages, rows, cols), F32), pltpu.SemaphoreType.DMA((2,))]


def _da_decode(pt, q3, kn3, cache, lam, g, l, lam_init, NS, n_pages):
    gs = pltpu.PrefetchScalarGridSpec(
        num_scalar_prefetch=1, grid=(NS,),
        in_specs=[pl.BlockSpec((1, DA_HEADS, LANE), lambda s, pt: (s, 0, 0)),
                  pl.BlockSpec((1, 1, DA_ROW), lambda s, pt: (s, 0, 0)),
                  pl.BlockSpec((1, 4, DA_HALF), lambda s, pt: (l, 0, 0)),
                  pl.BlockSpec((1, 1, DA_DV), lambda s, pt: (l, 0, 0)),
                  pl.BlockSpec(memory_space=pl.ANY)],
        out_specs=pl.BlockSpec((1, DA_HEADS, DA_DV), lambda s, pt: (s, 0, 0)),
        scratch_shapes=_decode_buffers(n_pages, PAGE, DA_ROW))
    return pl.pallas_call(
        functools.partial(_da_dec_kernel, n_pages=n_pages, l=l, lam_init=lam_init),
        grid_spec=gs, out_shape=jax.ShapeDtypeStruct((NS, DA_HEADS, DA_DV), F32),
        compiler_params=_cp(("arbitrary",)), name="da_decode")(pt, q3, kn3, lam, g, cache)


def _mla_dec_kernel(pt_ref, ql_ref, qr_ref, kn_ref, cache_hbm, o_ref, buf, sem, *, n_pages, l):
    slot = _stream_pages(pt_ref, cache_hbm, buf, sem, l, n_pages)
    R = MLA_KV_RANK
    zpad = jnp.zeros((8 - MLA_HEADS, R), F32)
    qlf = jnp.concatenate([ql_ref[0], zpad], axis=0)
    qrf = jnp.concatenate([qr_ref[0], zpad[:, 0:MLA_ROPE]], axis=0)
    qs = jnp.concatenate([qlf, qrf], axis=1).astype(BF16)
    scale = (MLA_NOPE + MLA_ROPE) ** -0.5
    kn = kn_ref[0]
    kv_t = jnp.concatenate([buf[slot, n].astype(BF16) for n in range(n_pages)], axis=1)
    s = _dot(qs, kv_t) * scale
    sn = (jnp.sum(qlf * kn[:, 0:R], axis=-1, keepdims=True)
          + jnp.sum(qrf * kn[:, R:MLA_ROW], axis=-1, keepdims=True)) * scale
    p, pn, den = _softmax_with_new(s, sn)
    pv = pn * kn[:, 0:R] + _dot_nt(p.astype(BF16), kv_t[0:R])
    o_ref[0] = (pv / den)[0:MLA_HEADS]


def _mla_decode(pt, ql3, qr3, kn3, cache_t, l, NS, n_pages):
    gs = pltpu.PrefetchScalarGridSpec(
        num_scalar_prefetch=1, grid=(NS,),
        in_specs=[pl.BlockSpec((1, MLA_HEADS, MLA_KV_RANK), lambda s, pt: (s, 0, 0)),
                  pl.BlockSpec((1, MLA_HEADS, MLA_ROPE), lambda s, pt: (s, 0, 0)),
                  pl.BlockSpec((1, 1, MLA_ROW), lambda s, pt: (s, 0, 0)),
                  pl.BlockSpec(memory_space=pl.ANY)],
        out_specs=pl.BlockSpec((1, MLA_HEADS, MLA_KV_RANK), lambda s, pt: (s, 0, 0)),
        scratch_shapes=_decode_buffers(n_pages, MLA_ROW, PAGE))
    return pl.pallas_call(
        functools.partial(_mla_dec_kernel, n_pages=n_pages, l=l),
        grid_spec=gs, out_shape=jax.ShapeDtypeStruct((NS, MLA_HEADS, MLA_KV_RANK), F32),
        compiler_params=_cp(("arbitrary",)), name="mla_decode")(pt, ql3, qr3, kn3, cache_t)


def _uv_kernel(o_ref, w_ref, y_ref):
    o = o_ref[...].astype(BF16)
    R = MLA_KV_RANK
    for h in range(MLA_HEADS):
        y_ref[:, h * LANE:(h + 1) * LANE] = _dot(o[:, h * R:(h + 1) * R],
                                                 w_ref[0, :, h * LANE:(h + 1) * LANE])


def _uv_sample(o_lat2, wuv, l):
    NS = o_lat2.shape[0]
    return pl.pallas_call(
        _uv_kernel, grid=(1,),
        in_specs=[pl.BlockSpec((NS, MLA_HEADS * MLA_KV_RANK), lambda i: (0, 0)),
                  pl.BlockSpec((1, MLA_KV_RANK, MLA_HEADS * MLA_DV), lambda i: (l, 0, 0))],
        out_specs=pl.BlockSpec((NS, MLA_HEADS * MLA_DV), lambda i: (0, 0)),
        out_shape=jax.ShapeDtypeStruct((NS, MLA_HEADS * MLA_DV), F32),
        compiler_params=_cp(("arbitrary",)), name="mla_uv")(o_lat2, wuv)


def _outproj_kernel(oa_ref, ob_ref, oc_ref, wa_ref, wb_ref, wc_ref, x_ref, gate_ref, o_ref):
    acc = (_dot(oa_ref[...].astype(BF16), wa_ref[0]) + _dot(ob_ref[...].astype(BF16), wb_ref[0])
           + _dot(oc_ref[...].astype(BF16), wc_ref[0]))
    o_ref[...] = x_ref[...] + gate_ref[0] * acc


def _outproj(oa, ob, oc, w, x, mod, mod_idx, mod_rows, l, tm):
    M = x.shape[0]
    D = D_MODEL
    tn = 512
    nj = D // tn
    return pl.pallas_call(
        _outproj_kernel,
        grid=(M // tm, nj),
        in_specs=[pl.BlockSpec((tm, 1024), lambda i, j: (i, 0)),
                  pl.BlockSpec((tm, 512), lambda i, j: (i, 0)),
                  pl.BlockSpec((tm, 512), lambda i, j: (i, 0)),
                  pl.BlockSpec((1, 1024, tn), lambda i, j: (l, 0, j)),
                  pl.BlockSpec((1, 512, tn), lambda i, j: (l, 2, j)),
                  pl.BlockSpec((1, 512, tn), lambda i, j: (l, 3, j)),
                  pl.BlockSpec((tm, tn), lambda i, j: (i, j)),
                  pl.BlockSpec((1, mod_rows, tn), lambda i, j: (mod_idx(i), 0, 2 * nj + j))],
        out_specs=pl.BlockSpec((tm, tn), lambda i, j: (i, j)),
        out_shape=jax.ShapeDtypeStruct((M, D), F32),
        compiler_params=_cp(("parallel", "parallel")), name="outproj")(oa, ob, oc, w, w, w, x, mod)


def _ffn_begin(x_ref, g_ref, sh_ref, sc_ref, h_scr, acc_scr):
    @pl.when(pl.program_id(1) == 0)
    def _():
        h = _rms(x_ref[...], g_ref[0]) * (1.0 + sc_ref[0]) + sh_ref[0]
        h_scr[...] = h.astype(BF16)
        acc_scr[...] = jnp.zeros_like(acc_scr)


def _ffn_cols(h_scr, wa_ref, wl_ref, cw_ref, cb_ref, wd_ref, cols, prev_rows):
    h = h_scr[...]
    a = _dot(h, wa_ref[0, :, cols])
    lin = _dot(h, wl_ref[0, :, cols])
    am1, am2 = prev_rows(a, cols)
    cw = cw_ref[0, :, cols]
    conv = cb_ref[0, :, cols] + am2 * cw[0:1] + am1 * cw[1:2] + a * cw[2:3]
    act = _silu(conv) * lin
    return _dot(act.astype(BF16), wd_ref[0, cols, :])


def _ffn_end(parts, x_ref, gate_ref, o_ref, acc_scr):
    total = parts[0]
    for part in parts[1:]:
        total = total + part
    acc_scr[...] += total

    @pl.when(pl.program_id(1) == pl.num_programs(1) - 1)
    def _():
        o_ref[...] = x_ref[...] + gate_ref[0] * acc_scr[...]


def _col_groups(tf):
    half = tf // 2
    return [slice(0, half), slice(half, tf)]


def _ffn_prompt_kernel(x_ref, g_ref, sh_ref, sc_ref, gate_ref, wa_ref, wl_ref, cw_ref, cb_ref, wd_ref,
                       o_ref, tail_ref, h_scr, acc_scr, carry_scr, *, tiles_per_seq):
    i = pl.program_id(0)
    f = pl.program_id(1)
    _ffn_begin(x_ref, g_ref, sh_ref, sc_ref, h_scr, acc_scr)
    tm = x_ref.shape[0]

    @pl.when(i % tiles_per_seq == 0)
    def _():
        carry_scr[f] = jnp.zeros(carry_scr.shape[1:], F32)

    def prev_rows(a, cols):
        prev = carry_scr[f, :, cols]
        row = lax.broadcasted_iota(jnp.int32, a.shape, 0)
        am1 = jnp.where(row == 0, prev[7:8], pltpu.roll(a, 1, axis=0))
        am2 = jnp.where(row == 0, prev[6:7], jnp.where(row == 1, prev[7:8], pltpu.roll(a, 2, axis=0)))
        last = a[tm - 8:tm]
        carry_scr[f, :, cols] = last
        tail_ref[0, :, cols] = last
        return am1, am2

    parts = [_ffn_cols(h_scr, wa_ref, wl_ref, cw_ref, cb_ref, wd_ref, cols, prev_rows)
             for cols in _col_groups(wa_ref.shape[2])]
    _ffn_end(parts, x_ref, gate_ref, o_ref, acc_scr)


def _ffn_sample_kernel(x_ref, g_ref, sh_ref, sc_ref, gate_ref, wa_ref, wl_ref, cw_ref, cb_ref, wd_ref,
                       st_ref, o_ref, tail_ref, h_scr, acc_scr):
    _ffn_begin(x_ref, g_ref, sh_ref, sc_ref, h_scr, acc_scr)

    def prev_rows(a, cols):
        am2 = st_ref[0, 0, :, cols]
        am1 = st_ref[0, 1, :, cols]
        tail_ref[0, :, cols] = am1
        tail_ref[1, :, cols] = a
        return am1, am2

    parts = [_ffn_cols(h_scr, wa_ref, wl_ref, cw_ref, cb_ref, wd_ref, cols, prev_rows)
             for cols in _col_groups(wa_ref.shape[2])]
    _ffn_end(parts, x_ref, gate_ref, o_ref, acc_scr)


def _ffn(x, g, mod, mod_idx, mod_rows, w_up, conv_w, conv_b, w_down, l, tm, seq_len=None, state=None):
    M = x.shape[0]
    D = D_MODEL
    tf = 512
    nf = D_FF // tf
    in_specs = [pl.BlockSpec((tm, D), lambda i, f: (i, 0)),
                pl.BlockSpec((1, 1, D), lambda i, f: (l, 0, 0)),
                pl.BlockSpec((1, mod_rows, D), lambda i, f: (mod_idx(i), 0, 3)),
                pl.BlockSpec((1, mod_rows, D), lambda i, f: (mod_idx(i), 0, 4)),
                pl.BlockSpec((1, mod_rows, D), lambda i, f: (mod_idx(i), 0, 5)),
                pl.BlockSpec((1, D, tf), lambda i, f: (l, 0, f)),
                pl.BlockSpec((1, D, tf), lambda i, f: (l, 0, nf + f)),
                pl.BlockSpec((1, 3, tf), lambda i, f: (l, 0, f)),
                pl.BlockSpec((1, 1, tf), lambda i, f: (l, 0, f)),
                pl.BlockSpec((1, tf, D), lambda i, f: (l, f, 0))]
    args = [x, g, mod, mod, mod, w_up, w_up, conv_w, conv_b, w_down]
    scratch = [pltpu.VMEM((tm, D), BF16), pltpu.VMEM((tm, D), F32)]
    if state is None:
        kern = functools.partial(_ffn_prompt_kernel, tiles_per_seq=seq_len // tm)
        tail_spec = pl.BlockSpec((1, 8, tf), lambda i, f: (i, 0, f))
        tail_shape = jax.ShapeDtypeStruct((M // tm, 8, D_FF), F32)
        scratch = scratch + [pltpu.VMEM((nf, 8, tf), F32)]
    else:
        kern = _ffn_sample_kernel
        in_specs.append(pl.BlockSpec((1, 2, tm, tf), lambda i, f: (l, 0, 0, f)))
        args.append(state)
        tail_spec = pl.BlockSpec((2, tm, tf), lambda i, f: (0, 0, f))
        tail_shape = jax.ShapeDtypeStruct((2, M, D_FF), F32)
    return pl.pallas_call(
        kern, grid=(M // tm, nf), in_specs=in_specs,
        out_specs=[pl.BlockSpec((tm, D), lambda i, f: (i, 0)), tail_spec],
        out_shape=[jax.ShapeDtypeStruct((M, D), F32), tail_shape],
        scratch_shapes=scratch,
        compiler_params=_cp(("arbitrary", "arbitrary")), name="convffn")(*args)


def _rope_tables(pos):
    inv = ROPE_THETA ** (-jnp.arange(0, 64, 2, dtype=F32) / 64)
    ang = pos.astype(F32)[:, None] * inv[None, :]
    cos, sin = jnp.cos(ang), jnp.sin(ang)
    cos2 = jnp.tile(cos, (1, 4))
    sin2 = jnp.tile(jnp.concatenate([-sin, sin], axis=1), (1, 2))
    return cos2, sin2


def kernel(x_prompt, x_sample, cache_diff_kv, cache_mla_latent, state_hgrn, state_ffn_conv, page_table, c_prompt, c_sample, norm_mix_g, norm_ffn_g, w_ada, b_ada, w_in, hg_lb_logits, hg_norm_g, da_lam, da_qnorm_g, da_knorm_g, da_onorm_g, mla_cq_norm_g, mla_ckv_norm_g, mla_w_uq, mla_w_uk, mla_w_uv, mla_qnorm_g, mla_knorm_g, w_out, w_up, conv_w, conv_b, w_down):
    B, T, D = x_prompt.shape
    NS = x_sample.shape[0]
    L = w_in.shape[0]
    n_pages = page_table.shape[1]
    past_len = n_pages * PAGE
    assert x_sample.shape[1] == 1 and D == D_MODEL

    w_in_b = jnp.pad(w_in.astype(BF16), ((0, 0), (0, 0), (0, PROJ_PAD - PROJ_WIDTH)))
    w_out_b = w_out.astype(BF16)
    w_up_b = w_up.astype(BF16)
    w_down_b = w_down.astype(BF16)
    wq4 = mla_w_uq.reshape(L, MLA_Q_RANK, MLA_HEADS, MLA_NOPE + MLA_ROPE)
    wuq_b = jnp.concatenate([wq4[..., :MLA_NOPE].reshape(L, MLA_Q_RANK, -1),
                             wq4[..., MLA_NOPE:].reshape(L, MLA_Q_RANK, -1)], axis=-1).astype(BF16)
    wuk_b = mla_w_uk.astype(BF16)
    wuv_b = mla_w_uv.astype(BF16)

    def vec(a):
        return a.reshape(L, 1, -1)

    gains = (vec(jnp.tile(da_qnorm_g, (1, 2))), vec(jnp.tile(da_knorm_g, (1, 2))),
             vec(mla_cq_norm_g), vec(mla_ckv_norm_g), vec(mla_qnorm_g[:, :MLA_NOPE]),
             vec(jnp.tile(mla_qnorm_g[:, MLA_NOPE:], (1, 2))),
             vec(jnp.pad(mla_knorm_g, ((0, 0), (0, LANE - MLA_ROPE)))))
    g_mix, g_ffn = vec(norm_mix_g), vec(norm_ffn_g)
    g_hg, g_da = vec(hg_norm_g), vec(da_onorm_g)
    conv_b3 = vec(conv_b)
    conv_state = state_ffn_conv.transpose(0, 2, 1, 3)
    cache_mla_t = jnp.swapaxes(cache_mla_latent, 2, 3)

    lb = _lower_bounds(hg_lb_logits)
    lb_p = lb.reshape(L, 1, HG_HEADS * HG_DK)
    lb_s = lb.reshape(L, HG_HEADS, HG_DK)

    n_rows = -(-(NS + B) // 8) * 8
    c_all = jnp.concatenate([c_sample, c_prompt, jnp.zeros((n_rows - NS - B, D), F32)], axis=0)
    mod_s = _ada(c_all, w_ada, vec(b_ada))
    mod_p = mod_s[:, NS:NS + B].reshape(L * B, 1, 6 * D)

    cos_p, sin_p = _rope_tables(jnp.arange(T))
    cos_s, sin_s = _rope_tables(jnp.full((NS,), past_len))
    pt_flat = page_table.reshape(-1)

    tm_p = min(1024, T)
    tm_ffn = min(512, T)
    tm_prep = min(256, T)
    xp = x_prompt.reshape(B * T, D)
    xs = x_sample.reshape(NS, D)
    outs = [[] for _ in range(7)]
    st_s_all = jnp.zeros(state_hgrn.shape, F32)
    for l in range(L):
        lam_init = 0.8 - 0.6 * math.exp(-0.3 * l)

        def idx_p(tm):
            return lambda i: l * B + (i * tm) // T

        def idx_s(i):
            return l

        proj = _inproj(xp, g_mix, mod_p, idx_p(tm_p), 1, w_in_b, l, tm_p)
        o_a, st_p = _hgrn_prompt(proj, lb_p, g_hg, l, B, T)
        q_da, row_da, q_lat, q_rope, row_mla = _prep(
            proj, cos_p, sin_p, T // tm_prep, gains, wuq_b, wuk_b, l, tm_prep)
        o_b = _da_attn(q_da, row_da.reshape(B, T, DA_ROW), da_lam, g_da, l, lam_init, B, T)
        o_c = _mla_attn(q_lat, q_rope, row_mla.reshape(B, T, MLA_ROW), wuv_b, l, B, T)
        xp = _outproj(o_a, o_b, o_c, w_out_b, xp, mod_p, idx_p(tm_p), 1, l, tm_p)
        xp, tail_p = _ffn(xp, g_ffn, mod_p, idx_p(tm_ffn), 1, w_up_b, conv_w, conv_b3, w_down_b, l,
                          tm_ffn, seq_len=T)
        tps = T // tm_ffn
        cv_p = tail_p.reshape(B, tps, 8, D_FF)[:, tps - 1, 6:8]

        proj_s = _inproj(xs, g_mix, mod_s, idx_s, NS, w_in_b, l, NS)
        oa_s, st_s_all = _hgrn_step(proj_s.reshape(NS, PROJ_PAD // LANE, LANE), lb_s, g_hg, state_hgrn,
                                    st_s_all, l, NS)
        qda_s, rda_s, qlat_s, qrope_s, rmla_s = _prep(
            proj_s, cos_s, sin_s, 1, gains, wuq_b, wuk_b, l, NS)
        ob_s = _da_decode(pt_flat, qda_s.reshape(NS, DA_HEADS, LANE), rda_s.reshape(NS, 1, DA_ROW),
                          cache_diff_kv, da_lam, g_da, l, lam_init, NS, n_pages)
        olat_s = _mla_decode(pt_flat, qlat_s.reshape(NS, MLA_HEADS, MLA_KV_RANK),
                             qrope_s.reshape(NS, MLA_HEADS, MLA_ROPE), rmla_s.reshape(NS, 1, MLA_ROW),
                             cache_mla_t, l, NS, n_pages)
        oc_s = _uv_sample(olat_s.reshape(NS, MLA_HEADS * MLA_KV_RANK), wuv_b, l)
        xs = _outproj(oa_s.reshape(NS, HG_HEADS * HG_DV), ob_s.reshape(NS, DA_HEADS * DA_DV), oc_s,
                      w_out_b, xs, mod_s, idx_s, NS, l, NS)
        xs, tail_s = _ffn(xs, g_ffn, mod_s, idx_s, NS, w_up_b, conv_w, conv_b3, w_down_b, l, NS,
                          state=conv_state)

        for lst, val in zip(outs, (row_da.reshape(B, T, DA_ROW), rda_s.reshape(NS, 1, DA_ROW),
                                   row_mla.reshape(B, T, MLA_ROW), rmla_s.reshape(NS, 1, MLA_ROW),
                                   st_p, cv_p, tail_s.transpose(1, 0, 2))):
            lst.append(val)

    da_p, da_s, mla_p, mla_s, hg_p, cv_p_all, cv_s_all = (jnp.stack(o) for o in outs)
    return (xp.reshape(B, T, D), xs.reshape(NS, 1, D), da_p, da_s, mla_p, mla_s, hg_p, st_s_all,
            cv_p_all, cv_s_all)
```

```python
import functools
import math

import jax
import jax.numpy as jnp
from jax import lax
from jax.experimental import pallas as pl
from jax.experimental.pallas import tpu as pltpu

F32 = jnp.float32
BF16 = jnp.bfloat16

D_MODEL = 2048
HG_HEADS = 8
HG_DK = 128
HG_DV = 128
DA_HEADS = 4
DA_HALF = 64
DA_DV = 128
DA_ROW = 256
MLA_HEADS = 4
MLA_Q_RANK = 512
MLA_KV_RANK = 256
MLA_NOPE = 128
MLA_ROPE = 64
MLA_DV = 128
MLA_ROW = MLA_KV_RANK + MLA_ROPE
D_FF = 5632
PAGE = 128
ROPE_THETA = 10000.0
EPS = 1e-6
NEG = -1e30
F_FLOOR = 1e-30
LOG2E = 1.4426950408889634
PROJ_WIDTH = 5696
PROJ_PAD = 6144
LANE = 128
HG_SUB = 16
DEC_BUF_BYTES = 24 * 1024 * 1024
VMEM_LIMIT = 48 * 1024 * 1024


def _cp(sem, vmem=VMEM_LIMIT):
    return pltpu.CompilerParams(dimension_semantics=sem, vmem_limit_bytes=vmem)


def _sigmoid(x):
    return 1.0 / (1.0 + jnp.exp(-x))


def _silu(x):
    return x * _sigmoid(x)


def _rms(x, g):
    ms = jnp.mean(x * x, axis=-1, keepdims=True)
    return x * lax.rsqrt(ms + EPS) * g


def _rms64(x, g2):
    lo = lax.broadcasted_iota(jnp.int32, x.shape, 1) < 64
    x2 = x * x
    s_lo = jnp.sum(jnp.where(lo, x2, 0.0), axis=-1, keepdims=True)
    s_hi = jnp.sum(jnp.where(lo, 0.0, x2), axis=-1, keepdims=True)
    ms = jnp.where(lo, s_lo, s_hi) * (1.0 / 64.0)
    return x * lax.rsqrt(ms + EPS) * g2


def _rope64(x, cos2, sin2):
    lane = lax.broadcasted_iota(jnp.int32, x.shape, 1)
    first = (lane % 64) < 32
    sw = jnp.where(first, pltpu.roll(x, LANE - 32, axis=1), pltpu.roll(x, 32, axis=1))
    return x * cos2 + sw * sin2


def _dot(a, b):
    return jnp.dot(a, b, preferred_element_type=F32)


def _dot_nt(a, b):
    return lax.dot_general(a, b, (((1,), (1,)), ((), ())), preferred_element_type=F32)


def _dot_tn(a, b):
    return lax.dot_general(a, b, (((0,), (0,)), ((), ())), preferred_element_type=F32)


def _lb_kernel(x_ref, o_ref):
    x = x_ref[...]
    n = x.shape[0]
    m = jnp.max(x, axis=0, keepdims=True)
    e = jnp.exp(x - m)
    p = e / jnp.sum(e, axis=0, keepdims=True)
    rows = []
    c = None
    for l in range(n):
        c = p[l:l + 1] if c is None else c + p[l:l + 1]
        rows.append(c - p[0:1])
    o_ref[...] = jnp.concatenate(rows, axis=0)


def _lower_bounds(logits):
    return pl.pallas_call(
        _lb_kernel, out_shape=jax.ShapeDtypeStruct(logits.shape, F32), name="lower_bounds")(logits)


def _ada_kernel(c_ref, w_ref, b_ref, o_ref):
    a = _silu(c_ref[...]).astype(BF16)
    o_ref[0] = _dot(a, w_ref[0].astype(BF16)) + b_ref[0]


def _ada(c_all, w_ada, b_ada):
    L, _, N = w_ada.shape
    R = c_all.shape[0]
    tn = 1024
    return pl.pallas_call(
        _ada_kernel,
        grid=(L, N // tn),
        in_specs=[pl.BlockSpec((R, D_MODEL), lambda l, j: (0, 0)),
                  pl.BlockSpec((1, D_MODEL, tn), lambda l, j: (l, 0, j)),
                  pl.BlockSpec((1, 1, tn), lambda l, j: (l, 0, j))],
        out_specs=pl.BlockSpec((1, R, tn), lambda l, j: (l, 0, j)),
        out_shape=jax.ShapeDtypeStruct((L, R, N), F32),
        compiler_params=_cp(("parallel", "parallel")), name="adaln")(c_all, w_ada, b_ada)


def _inproj_kernel(x_ref, g_ref, sh_ref, sc_ref, w_ref, o_ref, h_scr):
    @pl.when(pl.program_id(1) == 0)
    def _():
        h = _rms(x_ref[...], g_ref[0]) * (1.0 + sc_ref[0]) + sh_ref[0]
        h_scr[...] = h.astype(BF16)

    o_ref[...] = _dot(h_scr[...], w_ref[0])


def _inproj(x, g, mod, mod_idx, mod_rows, w, l, tm):
    M = x.shape[0]
    NP = w.shape[2]
    tn = 512
    D = D_MODEL
    return pl.pallas_call(
        _inproj_kernel,
        grid=(M // tm, NP // tn),
        in_specs=[pl.BlockSpec((tm, D), lambda i, j: (i, 0)),
                  pl.BlockSpec((1, 1, D), lambda i, j: (l, 0, 0)),
                  pl.BlockSpec((1, mod_rows, D), lambda i, j: (mod_idx(i), 0, 0)),
                  pl.BlockSpec((1, mod_rows, D), lambda i, j: (mod_idx(i), 0, 1)),
                  pl.BlockSpec((1, D, tn), lambda i, j: (l, 0, j))],
        out_specs=pl.BlockSpec((tm, tn), lambda i, j: (i, j)),
        out_shape=jax.ShapeDtypeStruct((M, NP), F32),
        scratch_shapes=[pltpu.VMEM((tm, D), BF16)],
        compiler_params=_cp(("parallel", "arbitrary")), name="inproj")(x, g, mod, mod, w)


def _hgrn_kernel(q_ref, f_ref, i_ref, g_ref, lb_ref, gn_ref, o_ref, st_ref, s_scr, bk_scr, *, n_sub):
    c = HG_SUB
    t = pl.program_id(1)

    @pl.when(t == 0)
    def _():
        s_scr[...] = jnp.zeros_like(s_scr)

    gn = gn_ref[0]
    row = lax.broadcasted_iota(jnp.int32, (c, HG_DK), 0)
    ones = jnp.ones((HG_DK, LANE), BF16)

    def body(ci, carry):
        r0 = pl.multiple_of(ci * c, c)
        rows = pl.ds(r0, c)
        for h in range(HG_HEADS):
            cols = slice(h * HG_DK, (h + 1) * HG_DK)
            lb = lb_ref[0, :, cols]
            z = f_ref[rows, cols]
            q = q_ref[rows, cols]
            v = i_ref[rows, cols]
            f = lb + (1.0 - lb) * _sigmoid(z)
            k = (1.0 - lb) * _sigmoid(-z)
            b = jnp.log(jnp.maximum(f, F_FLOOR))
            s = 1
            while s < c:
                b = b + jnp.where(row >= s, pltpu.roll(b, s, axis=0), 0.0)
                s *= 2
            b = b * LOG2E
            bl = b[c - 1:c]
            st = s_scr[h]
            o = _dot_nt((q * jnp.exp2(b)).astype(BF16), st.astype(BF16))
            bk_scr[h, 0] = b
            bk_scr[h, 1] = k
            bk_scr[h, 2] = v
            w = jnp.concatenate(
                [(jnp.where(row >= s, jnp.exp2(b - bk_scr[h, 0, s:s + 1, :]), 0.0)
                  * (q * bk_scr[h, 1, s:s + 1, :])).astype(BF16) for s in range(c)], axis=0)
            a = _dot(w, ones)
            for s in range(c):
                o = o + a[s * c:(s + 1) * c] * bk_scr[h, 2, s:s + 1, :]
            ke = k * jnp.exp2(bl - b)
            s_scr[h] = st * jnp.exp2(bl) + _dot_tn(v.astype(BF16), ke.astype(BF16))
            o_ref[rows, cols] = _rms(o, gn) * _silu(g_ref[rows, cols])
        return carry

    lax.fori_loop(0, n_sub, body, 0)

    @pl.when(t == pl.num_programs(1) - 1)
    def _():
        for h in range(HG_HEADS):
            st_ref[0, h] = s_scr[h].T


def _hgrn_prompt(proj, lb, gn, l, B, T):
    tc = min(256, T)
    nt = T // tc
    W = HG_HEADS * HG_DK

    def col(k):
        return pl.BlockSpec((tc, W), lambda b, t: (b * nt + t, k))

    return pl.pallas_call(
        functools.partial(_hgrn_kernel, n_sub=tc // HG_SUB),
        grid=(B, nt),
        in_specs=[col(0), col(1), col(2), col(3),
                  pl.BlockSpec((1, 1, W), lambda b, t: (l, 0, 0)),
                  pl.BlockSpec((1, 1, HG_DV), lambda b, t: (l, 0, 0))],
        out_specs=[pl.BlockSpec((tc, W), lambda b, t: (b * nt + t, 0)),
                   pl.BlockSpec((1, HG_HEADS, HG_DK, HG_DV), lambda b, t: (b, 0, 0, 0))],
        out_shape=[jax.ShapeDtypeStruct((B * T, HG_HEADS * HG_DV), F32),
                   jax.ShapeDtypeStruct((B, HG_HEADS, HG_DK, HG_DV), F32)],
        scratch_shapes=[pltpu.VMEM((HG_HEADS, HG_DV, HG_DK), F32),
                        pltpu.VMEM((HG_HEADS, 3, HG_SUB, HG_DK), F32)],
        compiler_params=_cp(("parallel", "arbitrary")),
        name="hgrn_prompt")(proj, proj, proj, proj, lb, gn)


def _hgrn_step_kernel(q_ref, f_ref, i_ref, g_ref, lb_ref, gn_ref, s_ref, all_ref, o_ref, so_ref):
    del all_ref
    lb = lb_ref[0]
    H = HG_HEADS
    pad = jnp.zeros((LANE - 2 * H, HG_DK), F32)
    for j in range(q_ref.shape[0]):
        z = f_ref[j]
        q = q_ref[j]
        v = i_ref[j]
        fd = jnp.maximum(lb + (1.0 - lb) * _sigmoid(z), F_FLOOR)
        k = (1.0 - lb) * _sigmoid(-z)
        cols = jnp.concatenate([k, fd, pad], axis=0).T
        qf = q * fd
        head = lax.broadcasted_iota(jnp.int32, qf.shape, 0)
        o = jnp.sum(q * k, axis=-1, keepdims=True) * v
        for h in range(H):
            kc = cols[:, h:h + 1]
            fc = cols[:, H + h:H + h + 1]
            s_old = s_ref[0, j, h]
            so_ref[0, j, h] = fc * s_old + kc * v[h:h + 1]
            o = o + _dot(jnp.where(head == h, qf, 0.0).astype(BF16), s_old.astype(BF16))
        o_ref[j] = _rms(o, gn_ref[0]) * _silu(g_ref[j])


def _hgrn_step(proj3, lb3, gn, state, new_state, l, NS):
    H, W = HG_HEADS, HG_DK
    sb = 4 if NS % 4 == 0 else 1

    def part(k):
        return pl.BlockSpec((sb, H, W), lambda s: (s, k, 0))

    in_specs = [part(0), part(1), part(2), part(3),
                pl.BlockSpec((1, H, W), lambda s: (l, 0, 0)),
                pl.BlockSpec((1, 1, W), lambda s: (l, 0, 0)),
                pl.BlockSpec((1, sb, H, HG_DK, HG_DV), lambda s: (l, s, 0, 0, 0)),
                pl.BlockSpec(memory_space=pl.ANY)]
    args = [proj3, proj3, proj3, proj3, lb3, gn, state, new_state]
    aliases = {len(args) - 1: 1}
    return pl.pallas_call(
        _hgrn_step_kernel,
        grid=(NS // sb,),
        in_specs=in_specs,
        out_specs=[pl.BlockSpec((sb, H, HG_DV), lambda s: (s, 0, 0)),
                   pl.BlockSpec((1, sb, H, HG_DK, HG_DV), lambda s: (l, s, 0, 0, 0))],
        out_shape=[jax.ShapeDtypeStruct((NS, H, HG_DV), F32),
                   jax.ShapeDtypeStruct(state.shape, F32)],
        input_output_aliases=aliases,
        compiler_params=_cp(("parallel",)), name="hgrn_step")(*args)


def _prep_kernel(dq_ref, dk_ref, dv_ref, cq0_ref, cq1_ref, ckv_ref, kr_ref, cos_ref, sin_ref,
                 gq_ref, gk_ref, gcq_ref, gckv_ref, gn_ref, gr_ref, gkr_ref, wuq_ref, wuk_ref,
                 qda_ref, rda_ref, qlat_ref, qrope_ref, rmla_ref):
    cos2 = cos_ref[...]
    sin2 = sin_ref[...]
    for h in range(DA_HEADS):
        x = dq_ref[:, h * LANE:(h + 1) * LANE]
        qda_ref[:, h * LANE:(h + 1) * LANE] = _rope64(_rms64(x, gq_ref[0]), cos2, sin2)
    rda_ref[:, 0:LANE] = _rope64(_rms64(dk_ref[...], gk_ref[0]), cos2, sin2)
    rda_ref[:, LANE:2 * LANE] = dv_ref[...]
    cq = jnp.concatenate([cq0_ref[...], cq1_ref[...]], axis=1)
    qh = _dot(_rms(cq, gcq_ref[0]).astype(BF16), wuq_ref[0])
    nope = [qh[:, h * LANE:(h + 1) * LANE] for h in range(MLA_HEADS)]
    ra = qh[:, 4 * LANE:5 * LANE]
    rb = qh[:, 5 * LANE:6 * LANE]
    lo = lax.broadcasted_iota(jnp.int32, ra.shape, 1) < 64

    def halves(x):
        x2 = x * x
        return (jnp.sum(jnp.where(lo, x2, 0.0), axis=-1, keepdims=True),
                jnp.sum(jnp.where(lo, 0.0, x2), axis=-1, keepdims=True))

    sr = halves(ra) + halves(rb)
    width = float(MLA_NOPE + MLA_ROPE)
    inv = [lax.rsqrt((jnp.sum(nope[h] * nope[h], axis=-1, keepdims=True) + sr[h]) / width + EPS)
           for h in range(MLA_HEADS)]
    for h in range(MLA_HEADS):
        qn = (nope[h] * inv[h] * gn_ref[0]).astype(BF16)
        qlat_ref[:, h * MLA_KV_RANK:(h + 1) * MLA_KV_RANK] = _dot_nt(
            qn, wuk_ref[0, :, h * LANE:(h + 1) * LANE])
    qrope_ref[:, 0:LANE] = _rope64(ra * jnp.where(lo, inv[0], inv[1]) * gr_ref[0], cos2, sin2)
    qrope_ref[:, LANE:2 * LANE] = _rope64(rb * jnp.where(lo, inv[2], inv[3]) * gr_ref[0], cos2, sin2)
    rmla_ref[:, 0:MLA_KV_RANK] = _rms(ckv_ref[...], gckv_ref[0])
    kr = kr_ref[...]
    ms = jnp.sum(kr * kr, axis=-1, keepdims=True) * (1.0 / MLA_ROPE)
    krr = _rope64(kr * lax.rsqrt(ms + EPS) * gkr_ref[0], cos2, sin2)
    rmla_ref[:, MLA_KV_RANK:MLA_ROW] = krr[:, 0:MLA_ROPE]


def _prep(proj, cos2, sin2, pos_tiles, gains, wuq, wuk, l, tm):
    M = proj.shape[0]
    gq, gk, gcq, gckv, gn, gr, gkr = gains

    def col(width, idx):
        return pl.BlockSpec((tm, width), lambda i: (i, idx))

    def vec(n):
        return pl.BlockSpec((1, 1, n), lambda i: (l, 0, 0))

    tab = pl.BlockSpec((tm, LANE), lambda i: (i % pos_tiles, 0))
    return pl.pallas_call(
        _prep_kernel,
        grid=(M // tm,),
        in_specs=[col(512, 8), col(128, 36), col(128, 37), col(256, 19), col(256, 20),
                  col(256, 21), col(128, 44), tab, tab,
                  vec(128), vec(128), vec(512), vec(256), vec(128), vec(128), vec(128),
                  pl.BlockSpec((1, MLA_Q_RANK, 768), lambda i: (l, 0, 0)),
                  pl.BlockSpec((1, MLA_KV_RANK, 512), lambda i: (l, 0, 0))],
        out_specs=[pl.BlockSpec((tm, 512), lambda i: (i, 0)),
                   pl.BlockSpec((tm, DA_ROW), lambda i: (i, 0)),
                   pl.BlockSpec((tm, MLA_HEADS * MLA_KV_RANK), lambda i: (i, 0)),
                   pl.BlockSpec((tm, MLA_HEADS * MLA_ROPE), lambda i: (i, 0)),
                   pl.BlockSpec((tm, MLA_ROW), lambda i: (i, 0))],
        out_shape=[jax.ShapeDtypeStruct((M, 512), F32),
                   jax.ShapeDtypeStruct((M, DA_ROW), F32),
                   jax.ShapeDtypeStruct((M, MLA_HEADS * MLA_KV_RANK), F32),
                   jax.ShapeDtypeStruct((M, MLA_HEADS * MLA_ROPE), F32),
                   jax.ShapeDtypeStruct((M, MLA_ROW), F32)],
        compiler_params=_cp(("parallel",)), name="prep")(
            proj, proj, proj, proj, proj, proj, proj, cos2, sin2,
            gq, gk, gcq, gckv, gn, gr, gkr, wuq, wuk)


def _lam_of(lam_ref, lam_init):
    lv = lam_ref[0]
    a = jnp.sum(lv[0:1] * lv[1:2], axis=-1, keepdims=True)
    b = jnp.sum(lv[2:3] * lv[3:4], axis=-1, keepdims=True)
    return jnp.exp(a) - jnp.exp(b) + lam_init


def _split_halves(q):
    lo = lax.broadcasted_iota(jnp.int32, q.shape, 1) < 64
    return jnp.concatenate([jnp.where(lo, q, 0.0), jnp.where(lo, 0.0, q)], axis=0)


def _lanes(x, width):
    return x if width == LANE else jnp.concatenate([x] * (width // LANE), axis=1)


def _online_update(s, v, m_scr, l_scr, acc_scr):
    m_old = m_scr[...]
    m_new = jnp.maximum(m_old, jnp.max(s, axis=-1, keepdims=True))
    alpha = jnp.exp2(m_old - m_new)
    p = jnp.exp2(s - _lanes(m_new, s.shape[1]))
    l_scr[...] = alpha * l_scr[...] + jnp.sum(p, axis=-1, keepdims=True)
    acc_scr[...] = _lanes(alpha, acc_scr.shape[1]) * acc_scr[...] + _dot(p.astype(BF16), v)
    m_scr[...] = m_new


def _causal_mask(s, tq, q0, k0):
    qpos = q0 + lax.broadcasted_iota(jnp.int32, s.shape, 0) % tq
    kpos = k0 + lax.broadcasted_iota(jnp.int32, s.shape, 1)
    return jnp.where(kpos <= qpos, s, NEG)


def _causal_sweep(i, tq, step):
    tk = 2 * tq
    n_full = i // 2

    def full(j, carry):
        step(pl.multiple_of(j * tk, tk), False)
        return carry

    lax.fori_loop(0, n_full, full, 0)
    step(pl.multiple_of(n_full * tk, tk), True)


def _da_attn_kernel(q_ref, kv_ref, lam_ref, g_ref, o_ref, m_scr, l_scr, acc_scr, *, tq, lam_init):
    i = pl.program_id(1)
    q = q_ref[...]
    qh = jnp.concatenate([q[:, h * LANE:(h + 1) * LANE] for h in range(DA_HEADS)], axis=0)
    qs = (_split_halves(qh) * (DA_HALF ** -0.5 * LOG2E)).astype(BF16)
    m_scr[...] = jnp.full_like(m_scr, NEG)
    l_scr[...] = jnp.zeros_like(l_scr)
    acc_scr[...] = jnp.zeros_like(acc_scr)

    def step(r0, masked):
        kv = kv_ref[0, pl.ds(r0, 2 * tq), :]
        s = _dot_nt(qs, kv[:, 0:LANE].astype(BF16))
        if masked:
            s = _causal_mask(s, tq, i * tq, r0)
        _online_update(s, kv[:, LANE:2 * LANE].astype(BF16), m_scr, l_scr, acc_scr)

    _causal_sweep(i, tq, step)
    o = acc_scr[...] / l_scr[...]
    n = DA_HEADS * tq
    od = o[0:n] - _lam_of(lam_ref, lam_init) * o[n:2 * n]
    for h in range(DA_HEADS):
        o_ref[:, h * LANE:(h + 1) * LANE] = _rms(od[h * tq:(h + 1) * tq], g_ref[0]) * (1.0 - lam_init)


def _da_attn(q, row3, lam, g, l, lam_init, B, T):
    tq = min(256, T // 2)
    nq = T // tq
    R = 2 * DA_HEADS * tq
    return pl.pallas_call(
        functools.partial(_da_attn_kernel, tq=tq, lam_init=lam_init),
        grid=(B, nq),
        in_specs=[pl.BlockSpec((tq, 512), lambda b, i: (b * nq + i, 0)),
                  pl.BlockSpec((1, T, DA_ROW), lambda b, i: (b, 0, 0)),
                  pl.BlockSpec((1, 4, DA_HALF), lambda b, i: (l, 0, 0)),
                  pl.BlockSpec((1, 1, DA_DV), lambda b, i: (l, 0, 0))],
        out_specs=pl.BlockSpec((tq, DA_HEADS * DA_DV), lambda b, i: (b * nq + i, 0)),
        out_shape=jax.ShapeDtypeStruct((B * T, DA_HEADS * DA_DV), F32),
        scratch_shapes=[pltpu.VMEM((R, LANE), F32), pltpu.VMEM((R, LANE), F32),
                        pltpu.VMEM((R, DA_DV), F32)],
        compiler_params=_cp(("parallel", "arbitrary")), name="da_attn")(q, row3, lam, g)


def _mla_attn_kernel(ql_ref, qr_ref, kv_ref, wuv_ref, o_ref, m_scr, l_scr, acc_scr, *, tq):
    i = pl.program_id(1)
    ql = ql_ref[...]
    qr = qr_ref[...]
    R = MLA_KV_RANK
    scale = (MLA_NOPE + MLA_ROPE) ** -0.5 * LOG2E
    qls = (jnp.concatenate([ql[:, h * R:(h + 1) * R] for h in range(MLA_HEADS)], axis=0)
           * scale).astype(BF16)
    qrs = (jnp.concatenate([qr[:, h * MLA_ROPE:(h + 1) * MLA_ROPE] for h in range(MLA_HEADS)], axis=0)
           * scale).astype(BF16)
    m_scr[...] = jnp.full_like(m_scr, NEG)
    l_scr[...] = jnp.zeros_like(l_scr)
    acc_scr[...] = jnp.zeros_like(acc_scr)

    def step(r0, masked):
        kv = kv_ref[0, pl.ds(r0, 2 * tq), :]
        lat = kv[:, 0:R].astype(BF16)
        s = _dot_nt(qls, lat) + _dot_nt(qrs, kv[:, R:MLA_ROW].astype(BF16))
        if masked:
            s = _causal_mask(s, tq, i * tq, r0)
        _online_update(s, lat, m_scr, l_scr, acc_scr)

    _causal_sweep(i, tq, step)
    o = (acc_scr[...] / _lanes(l_scr[...], R)).astype(BF16)
    for h in range(MLA_HEADS):
        o_ref[:, h * LANE:(h + 1) * LANE] = _dot(o[h * tq:(h + 1) * tq],
                                                 wuv_ref[0, :, h * LANE:(h + 1) * LANE])


def _mla_attn(ql, qr, row3, wuv, l, B, T):
    tq = min(256, T // 2)
    nq = T // tq
    R = MLA_HEADS * tq
    return pl.pallas_call(
        functools.partial(_mla_attn_kernel, tq=tq),
        grid=(B, nq),
        in_specs=[pl.BlockSpec((tq, MLA_HEADS * MLA_KV_RANK), lambda b, i: (b * nq + i, 0)),
                  pl.BlockSpec((tq, MLA_HEADS * MLA_ROPE), lambda b, i: (b * nq + i, 0)),
                  pl.BlockSpec((1, T, MLA_ROW), lambda b, i: (b, 0, 0)),
                  pl.BlockSpec((1, MLA_KV_RANK, MLA_HEADS * MLA_DV), lambda b, i: (l, 0, 0))],
        out_specs=pl.BlockSpec((tq, MLA_HEADS * MLA_DV), lambda b, i: (b * nq + i, 0)),
        out_shape=jax.ShapeDtypeStruct((B * T, MLA_HEADS * MLA_DV), F32),
        scratch_shapes=[pltpu.VMEM((R, LANE), F32), pltpu.VMEM((R, LANE), F32),
                        pltpu.VMEM((R, MLA_KV_RANK), F32)],
        compiler_params=_cp(("parallel", "arbitrary")), name="mla_attn")(ql, qr, row3, wuv)


def _page_copy(cache_hbm, buf, sem, l, page, slot, p):
    return pltpu.make_async_copy(cache_hbm.at[l, page], buf.at[slot, p], sem.at[slot])


def _stream_pages(pt_ref, cache_hbm, buf, sem, l, n_pages):
    s = pl.program_id(0)
    slot = s % 2

    def fetch(seq, to_slot):
        for p in range(n_pages):
            _page_copy(cache_hbm, buf, sem, l, pt_ref[seq * n_pages + p], to_slot, p).start(priority=p % 2)

    @pl.when(s == 0)
    def _():
        fetch(0, 0)

    @pl.when(s + 1 < pl.num_programs(0))
    def _():
        fetch(s + 1, 1 - slot)

    for p in range(n_pages):
        _page_copy(cache_hbm, buf, sem, l, 0, slot, p).wait()
    return slot


def _softmax_with_new(s, sn):
    m = jnp.maximum(jnp.max(s, axis=-1, keepdims=True), sn)
    p = jnp.exp(s - m)
    pn = jnp.exp(sn - m)
    return p, pn, jnp.sum(p, axis=-1, keepdims=True) + pn


def _da_dec_kernel(pt_ref, q_ref, kn_ref, lam_ref, g_ref, cache_hbm, o_ref, buf, sem, *,
                   n_pages, l, lam_init):
    slot = _stream_pages(pt_ref, cache_hbm, buf, sem, l, n_pages)
    qf = _split_halves(q_ref[0]) * (DA_HALF ** -0.5)
    qs = qf.astype(BF16)
    kn = kn_ref[0]
    kv = buf[slot].reshape(n_pages * PAGE, DA_ROW)
    s = _dot_nt(qs, kv[:, 0:LANE].astype(BF16))
    sn = jnp.sum(qf * kn[:, 0:LANE], axis=-1, keepdims=True)
    p, pn, den = _softmax_with_new(s, sn)
    pv = pn * kn[:, LANE:2 * LANE] + _dot(p.astype(BF16), kv[:, LANE:2 * LANE].astype(BF16))
    o = pv / den
    od = o[0:DA_HEADS] - _lam_of(lam_ref, lam_init) * o[DA_HEADS:2 * DA_HEADS]
    o_ref[0] = _rms(od, g_ref[0]) * (1.0 - lam_init)


def _decode_buffers(n_pages, rows, cols):
    assert 2 * n_pages * rows * cols * 4 <= DEC_BUF_BYTES, "KV pages of one sequence must fit the buffers"
    return [pltpu.VMEM((2, n_pages, rows, cols), F32), pltpu.SemaphoreType.DMA((2,))]


def _da_decode(pt, q3, kn3, cache, lam, g, l, lam_init, NS, n_pages):
    gs = pltpu.PrefetchScalarGridSpec(
        num_scalar_prefetch=1, grid=(NS,),
        in_specs=[pl.BlockSpec((1, DA_HEADS, LANE), lambda s, pt: (s, 0, 0)),
                  pl.BlockSpec((1, 1, DA_ROW), lambda s, pt: (s, 0, 0)),
                  pl.BlockSpec((1, 4, DA_HALF), lambda s, pt: (l, 0, 0)),
                  pl.BlockSpec((1, 1, DA_DV), lambda s, pt: (l, 0, 0)),
                  pl.BlockSpec(memory_space=pl.ANY)],
        out_specs=pl.BlockSpec((1, DA_HEADS, DA_DV), lambda s, pt: (s, 0, 0)),
        scratch_shapes=_decode_buffers(n_pages, PAGE, DA_ROW))
    return pl.pallas_call(
        functools.partial(_da_dec_kernel, n_pages=n_pages, l=l, lam_init=lam_init),
        grid_spec=gs, out_shape=jax.ShapeDtypeStruct((NS, DA_HEADS, DA_DV), F32),
        compiler_params=_cp(("arbitrary",)), name="da_decode")(pt, q3, kn3, lam, g, cache)


def _mla_dec_kernel(pt_ref, ql_ref, qr_ref, kn_ref, cache_hbm, o_ref, buf, sem, *, n_pages, l):
    slot = _stream_pages(pt_ref, cache_hbm, buf, sem, l, n_pages)
    R = MLA_KV_RANK
    zpad = jnp.zeros((8 - MLA_HEADS, R), F32)
    qlf = jnp.concatenate([ql_ref[0], zpad], axis=0)
    qrf = jnp.concatenate([qr_ref[0], zpad[:, 0:MLA_ROPE]], axis=0)
    qs = jnp.concatenate([qlf, qrf], axis=1).astype(BF16)
    scale = (MLA_NOPE + MLA_ROPE) ** -0.5
    kn = kn_ref[0]
    kv_t = jnp.concatenate([buf[slot, n].astype(BF16) for n in range(n_pages)], axis=1)
    s = _dot(qs, kv_t) * scale
    sn = (jnp.sum(qlf * kn[:, 0:R], axis=-1, keepdims=True)
          + jnp.sum(qrf * kn[:, R:MLA_ROW], axis=-1, keepdims=True)) * scale
    p, pn, den = _softmax_with_new(s, sn)
    pv = pn * kn[:, 0:R] + _dot_nt(p.astype(BF16), kv_t[0:R])
    o_ref[0] = (pv / den)[0:MLA_HEADS]


def _mla_decode(pt, ql3, qr3, kn3, cache_t, l, NS, n_pages):
    gs = pltpu.PrefetchScalarGridSpec(
        num_scalar_prefetch=1, grid=(NS,),
        in_specs=[pl.BlockSpec((1, MLA_HEADS, MLA_KV_RANK), lambda s, pt: (s, 0, 0)),
                  pl.BlockSpec((1, MLA_HEADS, MLA_ROPE), lambda s, pt: (s, 0, 0)),
                  pl.BlockSpec((1, 1, MLA_ROW), lambda s, pt: (s, 0, 0)),
                  pl.BlockSpec(memory_space=pl.ANY)],
        out_specs=pl.BlockSpec((1, MLA_HEADS, MLA_KV_RANK), lambda s, pt: (s, 0, 0)),
        scratch_shapes=_decode_buffers(n_pages, MLA_ROW, PAGE))
    return pl.pallas_call(
        functools.partial(_mla_dec_kernel, n_pages=n_pages, l=l),
        grid_spec=gs, out_shape=jax.ShapeDtypeStruct((NS, MLA_HEADS, MLA_KV_RANK), F32),
        compiler_params=_cp(("arbitrary",)), name="mla_decode")(pt, ql3, qr3, kn3, cache_t)


def _uv_kernel(o_ref, w_ref, y_ref):
    o = o_ref[...].astype(BF16)
    R = MLA_KV_RANK
    for h in range(MLA_HEADS):
        y_ref[:, h * LANE:(h + 1) * LANE] = _dot(o[:, h * R:(h + 1) * R],
                                                 w_ref[0, :, h * LANE:(h + 1) * LANE])


def _uv_sample(o_lat2, wuv, l):
    NS = o_lat2.shape[0]
    return pl.pallas_call(
        _uv_kernel, grid=(1,),
        in_specs=[pl.BlockSpec((NS, MLA_HEADS * MLA_KV_RANK), lambda i: (0, 0)),
                  pl.BlockSpec((1, MLA_KV_RANK, MLA_HEADS * MLA_DV), lambda i: (l, 0, 0))],
        out_specs=pl.BlockSpec((NS, MLA_HEADS * MLA_DV), lambda i: (0, 0)),
        out_shape=jax.ShapeDtypeStruct((NS, MLA_HEADS * MLA_DV), F32),
        compiler_params=_cp(("arbitrary",)), name="mla_uv")(o_lat2, wuv)


def _outproj_kernel(oa_ref, ob_ref, oc_ref, wa_ref, wb_ref, wc_ref, x_ref, gate_ref, o_ref):
    acc = (_dot(oa_ref[...].astype(BF16), wa_ref[0]) + _dot(ob_ref[...].astype(BF16), wb_ref[0])
           + _dot(oc_ref[...].astype(BF16), wc_ref[0]))
    o_ref[...] = x_ref[...] + gate_ref[0] * acc


def _outproj(oa, ob, oc, w, x, mod, mod_idx, mod_rows, l, tm):
    M = x.shape[0]
    D = D_MODEL
    tn = 512
    nj = D // tn
    return pl.pallas_call(
        _outproj_kernel,
        grid=(M // tm, nj),
        in_specs=[pl.BlockSpec((tm, 1024), lambda i, j: (i, 0)),
                  pl.BlockSpec((tm, 512), lambda i, j: (i, 0)),
                  pl.BlockSpec((tm, 512), lambda i, j: (i, 0)),
                  pl.BlockSpec((1, 1024, tn), lambda i, j: (l, 0, j)),
                  pl.BlockSpec((1, 512, tn), lambda i, j: (l, 2, j)),
                  pl.BlockSpec((1, 512, tn), lambda i, j: (l, 3, j)),
                  pl.BlockSpec((tm, tn), lambda i, j: (i, j)),
                  pl.BlockSpec((1, mod_rows, tn), lambda i, j: (mod_idx(i), 0, 2 * nj + j))],
        out_specs=pl.BlockSpec((tm, tn), lambda i, j: (i, j)),
        out_shape=jax.ShapeDtypeStruct((M, D), F32),
        compiler_params=_cp(("parallel", "parallel")), name="outproj")(oa, ob, oc, w, w, w, x, mod)


def _ffn_begin(x_ref, g_ref, sh_ref, sc_ref, h_scr, acc_scr):
    @pl.when(pl.program_id(1) == 0)
    def _():
        h = _rms(x_ref[...], g_ref[0]) * (1.0 + sc_ref[0]) + sh_ref[0]
        h_scr[...] = h.astype(BF16)
        acc_scr[...] = jnp.zeros_like(acc_scr)


def _ffn_cols(h_scr, wa_ref, wl_ref, cw_ref, cb_ref, wd_ref, cols, prev_rows):
    h = h_scr[...]
    a = _dot(h, wa_ref[0, :, cols])
    lin = _dot(h, wl_ref[0, :, cols])
    am1, am2 = prev_rows(a, cols)
    cw = cw_ref[0, :, cols]
    conv = cb_ref[0, :, cols] + am2 * cw[0:1] + am1 * cw[1:2] + a * cw[2:3]
    act = _silu(conv) * lin
    return _dot(act.astype(BF16), wd_ref[0, cols, :])


def _ffn_end(parts, x_ref, gate_ref, o_ref, acc_scr):
    total = parts[0]
    for part in parts[1:]:
        total = total + part
    acc_scr[...] += total

    @pl.when(pl.program_id(1) == pl.num_programs(1) - 1)
    def _():
        o_ref[...] = x_ref[...] + gate_ref[0] * acc_scr[...]


def _col_groups(tf):
    half = tf // 2
    return [slice(0, half), slice(half, tf)]


def _ffn_prompt_kernel(x_ref, g_ref, sh_ref, sc_ref, gate_ref, wa_ref, wl_ref, cw_ref, cb_ref, wd_ref,
                       o_ref, tail_ref, h_scr, acc_scr, carry_scr, *, tiles_per_seq):
    i = pl.program_id(0)
    f = pl.program_id(1)
    _ffn_begin(x_ref, g_ref, sh_ref, sc_ref, h_scr, acc_scr)
    tm = x_ref.shape[0]

    @pl.when(i % tiles_per_seq == 0)
    def _():
        carry_scr[f] = jnp.zeros(carry_scr.shape[1:], F32)

    def prev_rows(a, cols):
        prev = carry_scr[f, :, cols]
        row = lax.broadcasted_iota(jnp.int32, a.shape, 0)
        am1 = jnp.where(row == 0, prev[7:8], pltpu.roll(a, 1, axis=0))
        am2 = jnp.where(row == 0, prev[6:7], jnp.where(row == 1, prev[7:8], pltpu.roll(a, 2, axis=0)))
        last = a[tm - 8:tm]
        carry_scr[f, :, cols] = last
        tail_ref[0, :, cols] = last
        return am1, am2

    parts = [_ffn_cols(h_scr, wa_ref, wl_ref, cw_ref, cb_ref, wd_ref, cols, prev_rows)
             for cols in _col_groups(wa_ref.shape[2])]
    _ffn_end(parts, x_ref, gate_ref, o_ref, acc_scr)


def _ffn_sample_kernel(x_ref, g_ref, sh_ref, sc_ref, gate_ref, wa_ref, wl_ref, cw_ref, cb_ref, wd_ref,
                       st_ref, o_ref, tail_ref, h_scr, acc_scr):
    _ffn_begin(x_ref, g_ref, sh_ref, sc_ref, h_scr, acc_scr)

    def prev_rows(a, cols):
        am2 = st_ref[0, 0, :, cols]
        am1 = st_ref[0, 1, :, cols]
        tail_ref[0, :, cols] = am1
        tail_ref[1, :, cols] = a
        return am1, am2

    parts = [_ffn_cols(h_scr, wa_ref, wl_ref, cw_ref, cb_ref, wd_ref, cols, prev_rows)
             for cols in _col_groups(wa_ref.shape[2])]
    _ffn_end(parts, x_ref, gate_ref, o_ref, acc_scr)


def _ffn(x, g, mod, mod_idx, mod_rows, w_up, conv_w, conv_b, w_down, l, tm, seq_len=None, state=None):
    M = x.shape[0]
    D = D_MODEL
    tf = 512
    nf = D_FF // tf
    in_specs = [pl.BlockSpec((tm, D), lambda i, f: (i, 0)),
                pl.BlockSpec((1, 1, D), lambda i, f: (l, 0, 0)),
                pl.BlockSpec((1, mod_rows, D), lambda i, f: (mod_idx(i), 0, 3)),
                pl.BlockSpec((1, mod_rows, D), lambda i, f: (mod_idx(i), 0, 4)),
                pl.BlockSpec((1, mod_rows, D), lambda i, f: (mod_idx(i), 0, 5)),
                pl.BlockSpec((1, D, tf), lambda i, f: (l, 0, f)),
                pl.BlockSpec((1, D, tf), lambda i, f: (l, 0, nf + f)),
                pl.BlockSpec((1, 3, tf), lambda i, f: (l, 0, f)),
                pl.BlockSpec((1, 1, tf), lambda i, f: (l, 0, f)),
                pl.BlockSpec((1, tf, D), lambda i, f: (l, f, 0))]
    args = [x, g, mod, mod, mod, w_up, w_up, conv_w, conv_b, w_down]
    scratch = [pltpu.VMEM((tm, D), BF16), pltpu.VMEM((tm, D), F32)]
    if state is None:
        kern = functools.partial(_ffn_prompt_kernel, tiles_per_seq=seq_len // tm)
        tail_spec = pl.BlockSpec((1, 8, tf), lambda i, f: (i, 0, f))
        tail_shape = jax.ShapeDtypeStruct((M // tm, 8, D_FF), F32)
        scratch = scratch + [pltpu.VMEM((nf, 8, tf), F32)]
    else:
        kern = _ffn_sample_kernel
        in_specs.append(pl.BlockSpec((1, 2, tm, tf), lambda i, f: (l, 0, 0, f)))
        args.append(state)
        tail_spec = pl.BlockSpec((2, tm, tf), lambda i, f: (0, 0, f))
        tail_shape = jax.ShapeDtypeStruct((2, M, D_FF), F32)
    return pl.pallas_call(
        kern, grid=(M // tm, nf), in_specs=in_specs,
        out_specs=[pl.BlockSpec((tm, D), lambda i, f: (i, 0)), tail_spec],
        out_shape=[jax.ShapeDtypeStruct((M, D), F32), tail_shape],
        scratch_shapes=scratch,
        compiler_params=_cp(("arbitrary", "arbitrary")), name="convffn")(*args)


def _rope_tables(pos):
    inv = ROPE_THETA ** (-jnp.arange(0, 64, 2, dtype=F32) / 64)
    ang = pos.astype(F32)[:, None] * inv[None, :]
    cos, sin = jnp.cos(ang), jnp.sin(ang)
    cos2 = jnp.tile(cos, (1, 4))
    sin2 = jnp.tile(jnp.concatenate([-sin, sin], axis=1), (1, 2))
    return cos2, sin2


def kernel(x_prompt, x_sample, cache_diff_kv, cache_mla_latent, state_hgrn, state_ffn_conv, page_table, c_prompt, c_sample, norm_mix_g, norm_ffn_g, w_ada, b_ada, w_in, hg_lb_logits, hg_norm_g, da_lam, da_qnorm_g, da_knorm_g, da_onorm_g, mla_cq_norm_g, mla_ckv_norm_g, mla_w_uq, mla_w_uk, mla_w_uv, mla_qnorm_g, mla_knorm_g, w_out, w_up, conv_w, conv_b, w_down):
    B, T, D = x_prompt.shape
    NS = x_sample.shape[0]
    L = w_in.shape[0]
    n_pages = page_table.shape[1]
    past_len = n_pages * PAGE
    assert x_sample.shape[1] == 1 and D == D_MODEL

    w_in_b = jnp.pad(w_in.astype(BF16), ((0, 0), (0, 0), (0, PROJ_PAD - PROJ_WIDTH)))
    w_out_b = w_out.astype(BF16)
    w_up_b = w_up.astype(BF16)
    w_down_b = w_down.astype(BF16)
    wq4 = mla_w_uq.reshape(L, MLA_Q_RANK, MLA_HEADS, MLA_NOPE + MLA_ROPE)
    wuq_b = jnp.concatenate([wq4[..., :MLA_NOPE].reshape(L, MLA_Q_RANK, -1),
                             wq4[..., MLA_NOPE:].reshape(L, MLA_Q_RANK, -1)], axis=-1).astype(BF16)
    wuk_b = mla_w_uk.astype(BF16)
    wuv_b = mla_w_uv.astype(BF16)

    def vec(a):
        return a.reshape(L, 1, -1)

    gains = (vec(jnp.tile(da_qnorm_g, (1, 2))), vec(jnp.tile(da_knorm_g, (1, 2))),
             vec(mla_cq_norm_g), vec(mla_ckv_norm_g), vec(mla_qnorm_g[:, :MLA_NOPE]),
             vec(jnp.tile(mla_qnorm_g[:, MLA_NOPE:], (1, 2))),
             vec(jnp.pad(mla_knorm_g, ((0, 0), (0, LANE - MLA_ROPE)))))
    g_mix, g_ffn = vec(norm_mix_g), vec(norm_ffn_g)
    g_hg, g_da = vec(hg_norm_g), vec(da_onorm_g)
    conv_b3 = vec(conv_b)
    conv_state = state_ffn_conv.transpose(0, 2, 1, 3)
    cache_mla_t = jnp.swapaxes(cache_mla_latent, 2, 3)

    lb = _lower_bounds(hg_lb_logits)
    lb_p = lb.reshape(L, 1, HG_HEADS * HG_DK)
    lb_s = lb.reshape(L, HG_HEADS, HG_DK)

    n_rows = -(-(NS + B) // 8) * 8
    c_all = jnp.concatenate([c_sample, c_prompt, jnp.zeros((n_rows - NS - B, D), F32)], axis=0)
    mod_s = _ada(c_all, w_ada, vec(b_ada))
    mod_p = mod_s[:, NS:NS + B].reshape(L * B, 1, 6 * D)

    cos_p, sin_p = _rope_tables(jnp.arange(T))
    cos_s, sin_s = _rope_tables(jnp.full((NS,), past_len))
    pt_flat = page_table.reshape(-1)

    tm_p = min(1024, T)
    tm_ffn = min(512, T)
    tm_prep = min(256, T)
    xp = x_prompt.reshape(B * T, D)
    xs = x_sample.reshape(NS, D)
    outs = [[] for _ in range(7)]
    st_s_all = jnp.zeros(state_hgrn.shape, F32)
    for l in range(L):
        lam_init = 0.8 - 0.6 * math.exp(-0.3 * l)

        def idx_p(tm):
            return lambda i: l * B + (i * tm) // T

        def idx_s(i):
            return l

        proj = _inproj(xp, g_mix, mod_p, idx_p(tm_p), 1, w_in_b, l, tm_p)
        o_a, st_p = _hgrn_prompt(proj, lb_p, g_hg, l, B, T)
        q_da, row_da, q_lat, q_rope, row_mla = _prep(
            proj, cos_p, sin_p, T // tm_prep, gains, wuq_b, wuk_b, l, tm_prep)
        o_b = _da_attn(q_da, row_da.reshape(B, T, DA_ROW), da_lam, g_da, l, lam_init, B, T)
        o_c = _mla_attn(q_lat, q_rope, row_mla.reshape(B, T, MLA_ROW), wuv_b, l, B, T)
        xp = _outproj(o_a, o_b, o_c, w_out_b, xp, mod_p, idx_p(tm_p), 1, l, tm_p)
        xp, tail_p = _ffn(xp, g_ffn, mod_p, idx_p(tm_ffn), 1, w_up_b, conv_w, conv_b3, w_down_b, l,
                          tm_ffn, seq_len=T)
        tps = T // tm_ffn
        cv_p = tail_p.reshape(B, tps, 8, D_FF)[:, tps - 1, 6:8]

        proj_s = _inproj(xs, g_mix, mod_s, idx_s, NS, w_in_b, l, NS)
        oa_s, st_s_all = _hgrn_step(proj_s.reshape(NS, PROJ_PAD // LANE, LANE), lb_s, g_hg, state_hgrn,
                                    st_s_all, l, NS)
        qda_s, rda_s, qlat_s, qrope_s, rmla_s = _prep(
            proj_s, cos_s, sin_s, 1, gains, wuq_b, wuk_b, l, NS)
        ob_s = _da_decode(pt_flat, qda_s.reshape(NS, DA_HEADS, LANE), rda_s.reshape(NS, 1, DA_ROW),
                          cache_diff_kv, da_lam, g_da, l, lam_init, NS, n_pages)
        olat_s = _mla_decode(pt_flat, qlat_s.reshape(NS, MLA_HEADS, MLA_KV_RANK),
                             qrope_s.reshape(NS, MLA_HEADS, MLA_ROPE), rmla_s.reshape(NS, 1, MLA_ROW),
                             cache_mla_t, l, NS, n_pages)
        oc_s = _uv_sample(olat_s.reshape(NS, MLA_HEADS * MLA_KV_RANK), wuv_b, l)
        xs = _outproj(oa_s.reshape(NS, HG_HEADS * HG_DV), ob_s.reshape(NS, DA_HEADS * DA_DV), oc_s,
                      w_out_b, xs, mod_s, idx_s, NS, l, NS)
        xs, tail_s = _ffn(xs, g_ffn, mod_s, idx_s, NS, w_up_b, conv_w, conv_b3, w_down_b, l, NS,
                          state=conv_state)

        for lst, val in zip(outs, (row_da.reshape(B, T, DA_ROW), rda_s.reshape(NS, 1, DA_ROW),
                                   row_mla.reshape(B, T, MLA_ROW), rmla_s.reshape(NS, 1, MLA_ROW),
                                   st_p, cv_p, tail_s.transpose(1, 0, 2))):
            lst.append(val)

    da_p, da_s, mla_p, mla_s, hg_p, cv_p_all, cv_s_all = (jnp.stack(o) for o in outs)
    return (xp.reshape(B, T, D), xs.reshape(NS, 1, D), da_p, da_s, mla_p, mla_s, hg_p, st_s_all,
            cv_p_all, cv_s_all)
```
